```python
import jax, jax.numpy as jnp
from jax import lax
import numpy as np

D_MODEL = 1024
BATCH = 16
SEQ = 4096
DEPTH = 1

CHUNK = 64
SUB_CHUNK = 16
Q_BLOCK = 128
A_WIDTH = D_MODEL // 2
A_HEAD_DIM = 128
A_HEADS = A_WIDTH // A_HEAD_DIM
B_WIDTH = D_MODEL // 2
B_HEAD_DIM = 64
B_HEADS = B_WIDTH // B_HEAD_DIM
N_BRANCH = 2
D_FF = -(-(8 * D_MODEL) // (3 * 256)) * 256
N_IN = 4 * A_WIDTH + 3 * B_WIDTH + B_HEADS + N_BRANCH * D_MODEL
ALPHA = (2.0 * DEPTH) ** 0.25
BETA = (8.0 * DEPTH) ** -0.25
LN_EPS = 1e-5
RMS_EPS = 1e-6
N_MOD = 6

kernel_name = "hybrid_hgrn2_fox_deepnorm_adaln_block"


def layer_norm(x, w, b):
    xf = x.astype(jnp.float32)
    mu = jnp.mean(xf, axis=-1, keepdims=True)
    var = jnp.mean(jnp.square(xf - mu), axis=-1, keepdims=True)
    return ((xf - mu) * lax.rsqrt(var + LN_EPS) * w + b).astype(x.dtype)


def gated_linear_recurrence(q, k, v, logf):
    bsz, seq, heads, dk = q.shape
    dv = v.shape[-1]
    n_chunks = seq // CHUNK
    ns = CHUNK // SUB_CHUNK

    def to_chunks(t):
        return t.reshape(bsz, n_chunks, CHUNK, heads, t.shape[-1]).transpose(1, 0, 3, 2, 4)

    qc, kc, vc, lc = to_chunks(q), to_chunks(k), to_chunks(v), to_chunks(logf)
    bc = jnp.cumsum(lc, axis=3)
    tri = jnp.tril(jnp.ones((SUB_CHUNK, SUB_CHUNK), dtype=bool))
    later = jnp.tril(jnp.ones((ns, ns), dtype=bool), -1)
    eye = jnp.eye(ns, dtype=jnp.float32)

    def step(state, inp):
        qt, kt, vt, bt = inp
        o_inter = jnp.einsum('bhtk,bhkv->bhtv', qt * jnp.exp(bt), state)
        qs = qt.reshape(bsz, heads, ns, SUB_CHUNK, dk)
        ks = kt.reshape(bsz, heads, ns, SUB_CHUNK, dk)
        bs = bt.reshape(bsz, heads, ns, SUB_CHUNK, dk)
        diff = bs[:, :, :, :, None, :] - bs[:, :, :, None, :, :]
        diff = jnp.where(tri[:, :, None], diff, -jnp.inf)
        a_diag = jnp.sum(qs[:, :, :, :, None, :] * ks[:, :, :, None, :, :] * jnp.exp(diff), axis=-1)
        b_ref = bs[:, :, :, -1, :]
        eq = bs[:, :, :, None, :, :] - b_ref[:, :, None, :, None, :]
        eq = jnp.where(later[:, :, None, None], eq, -jnp.inf)
        qd = qs[:, :, :, None] * jnp.exp(eq)
        kd = ks * jnp.exp(b_ref[:, :, :, None, :] - bs)
        a_off = jnp.einsum('bhijtk,bhjsk->bhitjs', qd, kd)
        a = a_off + eye[:, None, :, None] * a_diag[:, :, :, :, None, :]
        a = a.reshape(bsz, heads, CHUNK, CHUNK)
        o = o_inter + jnp.einsum('bhts,bhsv->bhtv', a, vt)
        b_last = bt[:, :, -1, :]
        new_state = (jnp.exp(b_last)[..., None] * state
                     + jnp.einsum('bhsk,bhsv->bhkv', kt * jnp.exp(b_last[:, :, None, :] - bt), vt))
        return new_state, o

    state0 = jnp.zeros((bsz, heads, dk, dv), jnp.float32)
    _, oc = lax.scan(step, state0, (qc, kc, vc, bc))
    return oc.transpose(1, 0, 3, 2, 4).reshape(bsz, seq, heads, dv)


def hgrn2_mixer(q, f_logit, i_in, g, lb, norm_w):
    bsz, seq, _ = q.shape
    dt = q.dtype

    def split(t):
        return t.reshape(bsz, seq, A_HEADS, A_HEAD_DIM).astype(jnp.float32)

    lbh = lb.astype(jnp.float32).reshape(A_HEADS, A_HEAD_DIM)
    f = lbh + (1.0 - lbh) * jax.nn.sigmoid(split(f_logit))
    o = gated_linear_recurrence(split(q), 1.0 - f, split(i_in), jnp.log(f))
    o = o * lax.rsqrt(jnp.mean(jnp.square(o), axis=-1, keepdims=True) + RMS_EPS)
    o = o * norm_w.astype(jnp.float32).reshape(A_HEADS, A_HEAD_DIM) * jax.nn.sigmoid(split(g))
    return o.reshape(bsz, seq, A_WIDTH).astype(dt)


def forgetting_attention(q, k, v, f_logit, f_bias):
    bsz, seq, _ = q.shape
    dt = q.dtype

    def heads(t):
        return t.reshape(bsz, seq, B_HEADS, B_HEAD_DIM).transpose(0, 2, 1, 3)

    qh, kh, vh = heads(q), heads(k), heads(v)
    logf = jax.nn.log_sigmoid((f_logit + f_bias).astype(jnp.float32))
    cum = jnp.cumsum(logf, axis=1).transpose(0, 2, 1)
    scale = B_HEAD_DIM ** -0.5
    outs = []
    for blk in range(seq // Q_BLOCK):
        lo, hi = blk * Q_BLOCK, (blk + 1) * Q_BLOCK
        s = jnp.einsum('bhqd,bhkd->bhqk', qh[:, :, lo:hi], kh[:, :, :hi]).astype(jnp.float32) * scale
        s = s + cum[:, :, lo:hi, None] - cum[:, :, None, :hi]
        causal = (lo + jnp.arange(Q_BLOCK))[:, None] >= jnp.arange(hi)[None, :]
        p = jax.nn.softmax(jnp.where(causal, s, -jnp.inf), axis=-1)
        outs.append(jnp.einsum('bhqk,bhkd->bhqd', p.astype(dt), vh[:, :, :hi]))
    o = jnp.concatenate(outs, axis=2)
    return o.transpose(0, 2, 1, 3).reshape(bsz, seq, B_WIDTH)


def setup_inputs(seed: int = 0) -> dict:
    key = jax.random.key(seed)
    ks = jax.random.split(key, 20)
    L, D = DEPTH, D_MODEL
    nrm = lambda k, shape, s: jax.random.normal(k, shape, jnp.float32) * s
    col_scale = np.ones((N_IN,), np.float32)
    col_scale[2 * A_WIDTH:3 * A_WIDTH] = BETA
    col_scale[4 * A_WIDTH + 2 * B_WIDTH:4 * A_WIDTH + 3 * B_WIDTH] = BETA
    return {
        "x": nrm(ks[0], (BATCH, SEQ, D), 1.0),
        "c": nrm(ks[1], (BATCH, D), 1.0),
        "w_ada": nrm(ks[2], (L, D, N_MOD * D), 0.1 * D ** -0.5),
        "b_ada": nrm(ks[3], (L, N_MOD * D), 0.01),
        "w_in": nrm(ks[4], (L, D, N_IN), D ** -0.5) * jnp.asarray(col_scale),
        "fox_f_bias": nrm(ks[5], (L, B_HEADS), 0.1) + 2.0,
        "lb_logits": nrm(ks[6], (L + 1, A_WIDTH), 0.1),
        "hgrn_norm_w": 1.0 + nrm(ks[7], (L, A_WIDTH), 0.02),
        "w_branch_a": nrm(ks[8], (L, A_WIDTH, D), BETA * A_WIDTH ** -0.5),
        "w_branch_b": nrm(ks[9], (L, B_WIDTH, D), BETA * B_WIDTH ** -0.5),
        "w_out": nrm(ks[10], (L, D, D), BETA * D ** -0.5),
        "ln1_w": 1.0 + nrm(ks[11], (L, D), 0.02),
        "ln1_b": nrm(ks[12], (L, D), 0.01),
        "w_ffn_gate": nrm(ks[13], (L, D, D_FF), BETA * D ** -0.5),
        "w_ffn_up": nrm(ks[14], (L, D, D_FF), BETA * D ** -0.5),
        "w_ffn_down": nrm(ks[15], (L, D_FF, D), BETA * D_FF ** -0.5),
        "ln2_w": 1.0 + nrm(ks[16], (L, D), 0.02),
        "ln2_b": nrm(ks[17], (L, D), 0.01),
    }


def reference(x, c, w_ada, b_ada, w_in, fox_f_bias, lb_logits, hgrn_norm_w, w_branch_a, w_branch_b,
              w_out, ln1_w, ln1_b, w_ffn_gate, w_ffn_up, w_ffn_down, ln2_w, ln2_b):
    splits = list(np.cumsum([A_WIDTH] * 4 + [B_WIDTH] * 3 + [B_HEADS])[:])
    lower_bounds = jnp.cumsum(jax.nn.softmax(lb_logits.astype(jnp.float32), axis=0), axis=0)
    c_act = jax.nn.silu(c)
    for l in range(DEPTH):
        mod = (c_act @ w_ada[l] + b_ada[l])[:, None, :]
        sh1, sc1, g1, sh2, sc2, g2 = jnp.split(mod, N_MOD, axis=-1)
        h = x * (1.0 + sc1) + sh1
        proj = h @ w_in[l]
        aq, af, ai, ag, bq, bk, bv, bf, gates = jnp.split(proj, splits, axis=-1)
        ya = hgrn2_mixer(aq, af, ai, ag, lower_bounds[l], hgrn_norm_w[l])
        yb = forgetting_attention(bq, bk, bv, bf, fox_f_bias[l])
        gate_a, gate_b = jnp.split(jax.nn.sigmoid(gates), N_BRANCH, axis=-1)
        merged = gate_a * (ya @ w_branch_a[l]) + gate_b * (yb @ w_branch_b[l])
        x = layer_norm(ALPHA * x + (1.0 + g1) * (merged @ w_out[l]), ln1_w[l], ln1_b[l])
        h = x * (1.0 + sc2) + sh2
        ffn = (jax.nn.silu(h @ w_ffn_gate[l]) * (h @ w_ffn_up[l])) @ w_ffn_down[l]
        x = layer_norm(ALPHA * x + (1.0 + g2) * ffn, ln2_w[l], ln2_b[l])
    return x
```

```python
import functools

import numpy as np
import jax
import jax.numpy as jnp
from jax import lax
from jax.experimental import pallas as pl
from jax.experimental.pallas import tpu as pltpu

F32 = jnp.float32
BF16 = jnp.bfloat16

A_HEAD_DIM = 128
B_HEAD_DIM = 64
CHUNK = 64
LN_EPS = 1e-5
RMS_EPS = 1e-6
N_MOD = 6

LANES = 128
VMEM_LIMIT_BYTES = 56 * 1024 * 1024

TS_INPROJ = 512
TT_HGRN = 256
HGRN_HALVES = (32, 16)
HGRN_LEAF = 16
BQ_ATT = 256
BK_ATT = 256
TM_OUT = 512
TM_FFN = 512
NEG_BIG = -1e30


def _cparams(n_axes):
    return pltpu.CompilerParams(dimension_semantics=("arbitrary",) * n_axes,
                                vmem_limit_bytes=VMEM_LIMIT_BYTES)


def _const_spec(shape):
    nd = len(shape)
    return pl.BlockSpec(shape, lambda *_: (0,) * nd, pipeline_mode=pl.Buffered(1))


def _split3(x):
    hi = x.astype(BF16)
    r1 = x - hi.astype(F32)
    mid = r1.astype(BF16)
    lo = (r1 - mid.astype(F32)).astype(BF16)
    return hi, mid, lo


def _layer_norm(z, w, b):
    mu = jnp.mean(z, axis=-1, keepdims=True)
    zc = z - mu
    var = jnp.mean(zc * zc, axis=-1, keepdims=True)
    return zc * lax.rsqrt(var + LN_EPS) * w + b


def _ada_body(c_ref, w_ref, b_ref, o_ref):
    c = c_ref[...]
    act = c * jax.nn.sigmoid(c)
    o_ref[...] = jnp.dot(act.astype(BF16), w_ref[...], preferred_element_type=F32) + b_ref[...]


def _ada_mod(c, w_bf, b):
    bsz, d = c.shape
    n = w_bf.shape[1]
    tn = n // 4
    return pl.pallas_call(
        _ada_body,
        grid=(n // tn,),
        in_specs=[pl.BlockSpec((bsz, d), lambda j: (0, 0)),
                  pl.BlockSpec((d, tn), lambda j: (0, j)),
                  pl.BlockSpec((1, tn), lambda j: (0, j))],
        out_specs=pl.BlockSpec((bsz, tn), lambda j: (0, j)),
        out_shape=jax.ShapeDtypeStruct((bsz, n), F32),
        compiler_params=_cparams(1),
        name="ada_mod",
    )(c, w_bf, b)


def _inproj_body(x_ref, mod_ref, wa_ref, wb_ref, wf_ref, wg_ref, fb_ref, lbl_ref,
                 aq_ref, ak_ref, alf_ref, av_ref, aog_ref, bq_ref, bk_ref, bv_ref, cum_ref, gate_ref,
                 carry_ref, *, a_width, b_width, scale):
    @pl.when(pl.program_id(1) == 0)
    def _():
        carry_ref[...] = jnp.zeros_like(carry_ref)

    x = x_ref[0]
    sh = mod_ref[0, 0:1, :]
    sc = mod_ref[0, 1:2, :]
    h = (x * (1.0 + sc) + sh).astype(BF16)

    aw = a_width
    pa = jnp.dot(h, wa_ref[...], preferred_element_type=F32)
    l0 = lbl_ref[0:1, :]
    l1 = lbl_ref[1:2, :]
    lmax = jnp.maximum(l0, l1)
    e0 = jnp.exp(l0 - lmax)
    lb = e0 / (e0 + jnp.exp(l1 - lmax))
    f = lb + (1.0 - lb) * jax.nn.sigmoid(pa[:, aw:2 * aw])
    aq_ref[0] = pa[:, 0:aw].astype(BF16)
    ak_ref[0] = (1.0 - f).astype(BF16)
    alf_ref[0] = jnp.log(f)
    av_ref[0] = pa[:, 2 * aw:3 * aw].astype(BF16)
    aog_ref[0] = jax.nn.sigmoid(pa[:, 3 * aw:4 * aw]).astype(BF16)

    bw = b_width
    pb = jnp.dot(h, wb_ref[...], preferred_element_type=F32)
    bq_ref[0] = (pb[:, 0:bw] * scale).astype(BF16)
    bk_ref[0] = pb[:, bw:2 * bw].astype(BF16)
    bv_ref[0] = pb[:, 2 * bw:3 * bw].astype(BF16)

    pf = jnp.dot(h, wf_ref[...], preferred_element_type=F32) + fb_ref[...]
    lsg = jnp.minimum(pf, 0.0) - jnp.log(1.0 + jnp.exp(-jnp.abs(pf)))
    ts = lsg.shape[0]
    r = lax.broadcasted_iota(jnp.int32, (ts, ts), 0)
    cidx = lax.broadcasted_iota(jnp.int32, (ts, ts), 1)
    tri = jnp.where(r >= cidx, 1.0, 0.0).astype(BF16)
    hi, mid, lo = _split3(lsg)
    parts = jnp.dot(tri, jnp.concatenate([hi, mid, lo], axis=1), preferred_element_type=F32)
    cum = parts[:, 0:LANES] + parts[:, LANES:2 * LANES] + parts[:, 2 * LANES:3 * LANES] + carry_ref[...]
    cum_ref[0] = cum
    carry_ref[...] = cum[ts - 1:ts, :]

    pg = jnp.dot(h, wg_ref[...], preferred_element_type=F32)
    gate_ref[0] = jax.nn.sigmoid(pg).astype(BF16)


def _inproj(x, mod3, wa, wb, wf, wg, fb, lbl, *, a_width, b_width):
    bsz, seq, d = x.shape
    ts = min(TS_INPROJ, seq)
    n_gate = wg.shape[1]
    tok = lambda w: pl.BlockSpec((1, ts, w), lambda b, s: (b, s, 0))
    out_shapes = [
        jax.ShapeDtypeStruct((bsz, seq, a_width), BF16),
        jax.ShapeDtypeStruct((bsz, seq, a_width), BF16),
        jax.ShapeDtypeStruct((bsz, seq, a_width), F32),
        jax.ShapeDtypeStruct((bsz, seq, a_width), BF16),
        jax.ShapeDtypeStruct((bsz, seq, a_width), BF16),
        jax.ShapeDtypeStruct((bsz, seq, b_width), BF16),
        jax.ShapeDtypeStruct((bsz, seq, b_width), BF16),
        jax.ShapeDtypeStruct((bsz, seq, b_width), BF16),
        jax.ShapeDtypeStruct((bsz, seq, LANES), F32),
        jax.ShapeDtypeStruct((bsz, seq, n_gate), BF16),
    ]
    out_specs = [tok(a_width), tok(a_width), tok(a_width), tok(a_width), tok(a_width),
                 tok(b_width), tok(b_width), tok(b_width), tok(LANES), tok(n_gate)]
    body = functools.partial(_inproj_body, a_width=a_width, b_width=b_width,
                             scale=float(B_HEAD_DIM) ** -0.5)
    return pl.pallas_call(
        body,
        grid=(bsz, seq // ts),
        in_specs=[tok(d),
                  pl.BlockSpec((1, N_MOD, d), lambda b, s: (b, 0, 0)),
                  _const_spec(wa.shape), _const_spec(wb.shape), _const_spec(wf.shape),
                  _const_spec(wg.shape), _const_spec(fb.shape), _const_spec(lbl.shape)],
        out_specs=out_specs,
        out_shape=out_shapes,
        scratch_shapes=[pltpu.VMEM((1, LANES), F32)],
        compiler_params=_cparams(2),
        name="in_proj",
    )(x, mod3, wa, wb, wf, wg, fb, lbl)


def _hgrn_level_table(tt):
    t = np.arange(tt)[:, None]
    s = np.arange(tt)[None, :]
    lvl = np.full((tt, tt), -1, np.int32)
    same_chunk = (t // CHUNK) == (s // CHUNK)
    for i, m in enumerate(HGRN_HALVES):
        blk = 2 * m
        own = same_chunk & ((t // blk) == (s // blk)) & ((t % blk) >= m) & ((s % blk) < m)
        lvl[own] = i
    leaf = same_chunk & ((t // HGRN_LEAF) == (s // HGRN_LEAF)) & (t >= s)
    lvl[leaf] = len(HGRN_HALVES)
    return lvl


def _block_row(b, blk, r):
    t, l = b.shape
    x = b.reshape(t // blk, blk, l)
    g = x[:, r:r + 1, :]
    return jnp.broadcast_to(g, x.shape).reshape(t, l)


def _hgrn_body(q_ref, k_ref, lf_ref, v_ref, og_ref, nw_ref, lvl_ref, y_ref, st_ref, *, heads):
    @pl.when(pl.program_id(1) == 0)
    def _():
        st_ref[...] = jnp.zeros_like(st_ref)

    tt = q_ref.shape[1]
    hd = A_HEAD_DIM
    nch = tt // CHUNK
    lvl = lvl_ref[...]
    row = lax.broadcasted_iota(jnp.int32, (tt, hd), 0)

    r = lax.broadcasted_iota(jnp.int32, (tt, tt), 0)
    c = lax.broadcasted_iota(jnp.int32, (tt, tt), 1)
    tri = jnp.where((r >= c) & ((r // CHUNK) == (c // CHUNK)), 1.0, 0.0).astype(BF16)
    hi, mid, lo = _split3(lf_ref[0])
    b_all = (jnp.dot(tri, hi, preferred_element_type=F32)
             + jnp.dot(tri, mid, preferred_element_type=F32)
             + jnp.dot(tri, lo, preferred_element_type=F32))

    nt = (((1,), (1,)), ((), ()))
    for h in range(heads):
        sl = slice(hd * h, hd * (h + 1))
        q = q_ref[0, :, sl].astype(F32)
        k = k_ref[0, :, sl].astype(F32)
        v = v_ref[0, :, sl]
        b = b_all[:, sl]

        a = jnp.zeros((tt, tt), F32)
        for i, m in enumerate(HGRN_HALVES):
            g = _block_row(b, 2 * m, m - 1)
            later = (row % (2 * m)) >= m
            e = jnp.where(later, b - g, g - b)
            xm = jnp.where(later, q, k) * jnp.exp(e)
            qm = jnp.where(later, xm, 0.0).astype(BF16)
            km = jnp.where(later, 0.0, xm).astype(BF16)
            am = lax.dot_general(qm, km, nt, preferred_element_type=F32)
            a = jnp.where(lvl == i, am, a)
        g = _block_row(b, HGRN_LEAF, HGRN_LEAF // 2 - 1)
        e = b - g
        ql = (q * jnp.exp(e)).astype(BF16)
        kl = (k * jnp.exp(-e)).astype(BF16)
        al = lax.dot_general(ql, kl, nt, preferred_element_type=F32)
        a = jnp.where(lvl == len(HGRN_HALVES), al, a)
        o_intra = jnp.dot(a.astype(BF16), v, preferred_element_type=F32)

        st = st_ref[h]
        outs = []
        for ci in range(nch):
            rows = slice(CHUNK * ci, CHUNK * (ci + 1))
            bc = b[rows]
            b_last = bc[CHUNK - 1:CHUNK, :]
            qd = (q[rows] * jnp.exp(bc)).astype(BF16)
            o_inter = lax.dot_general(qd, st.astype(BF16), nt, preferred_element_type=F32)
            outs.append(o_inter + o_intra[rows])
            kd = (k[rows] * jnp.exp(b_last - bc)).astype(BF16)
            vt = v[rows].astype(F32).T.astype(BF16)
            st = st * jnp.exp(b_last) + jnp.dot(vt, kd, preferred_element_type=F32)
        st_ref[h] = st

        o = jnp.concatenate(outs, axis=0)
        ms = jnp.mean(o * o, axis=-1, keepdims=True)
        y = o * lax.rsqrt(ms + RMS_EPS) * nw_ref[:, sl] * og_ref[0, :, sl].astype(F32)
        y_ref[0, :, sl] = y.astype(BF16)


def _hgrn(aq, ak, alf, av, aog, norm_w):
    bsz, seq, width = aq.shape
    heads = width // A_HEAD_DIM
    tt = min(TT_HGRN, seq)
    lvl = jnp.asarray(_hgrn_level_table(tt))
    tok = pl.BlockSpec((1, tt, width), lambda b, s: (b, s, 0))
    return pl.pallas_call(
        functools.partial(_hgrn_body, heads=heads),
        grid=(bsz, seq // tt),
        in_specs=[tok, tok, tok, tok, tok, _const_spec((1, width)), _const_spec((tt, tt))],
        out_specs=tok,
        out_shape=jax.ShapeDtypeStruct((bsz, seq, width), BF16),
        scratch_shapes=[pltpu.VMEM((heads, A_HEAD_DIM, A_HEAD_DIM), F32)],
        compiler_params=_cparams(2),
        name="hgrn2",
    )(aq, ak, alf, av, aog, norm_w, lvl)


_CQ_LANES = (64, 65, 66)
_CK_LANES = (67, 68, 69)
_ONE_LANE_V = 64


def _fox_body(q_ref, k_ref, v_ref, cum_ref, o_ref, qa_ref, ka_ref, vt_ref, *, bq, bk):
    seq = q_ref.shape[1]
    hd = B_HEAD_DIM
    pair = pl.program_id(1)
    lane = lax.broadcasted_iota(jnp.int32, (seq, LANES), 1)
    nt = (((1,), (1,)), ((), ()))

    c_hi, c_mid, c_lo = _split3(cum_ref[0])
    c3 = jnp.concatenate([c_hi, c_mid, c_lo], axis=1)
    pr = lax.broadcasted_iota(jnp.int32, (3 * LANES, LANES), 0)
    pc = lax.broadcasted_iota(jnp.int32, (3 * LANES, LANES), 1)
    ones_q = ((lane >= _CK_LANES[0]) & (lane <= _CK_LANES[2])).astype(F32)
    ones_k = ((lane >= _CQ_LANES[0]) & (lane <= _CQ_LANES[2])).astype(F32)
    ones_v = (lane == _ONE_LANE_V).astype(F32)
    for j in range(2):
        head = 2 * pair + j
        sel_q = jnp.zeros((3 * LANES, LANES), F32)
        sel_k = jnp.zeros((3 * LANES, LANES), F32)
        for part in range(3):
            src = pr == (part * LANES + head)
            sel_q = jnp.where(src & (pc == _CQ_LANES[part]), 1.0, sel_q)
            sel_k = jnp.where(src & (pc == _CK_LANES[part]), -1.0, sel_k)
        aug_q = jnp.dot(c3, sel_q.astype(BF16), preferred_element_type=F32) + ones_q
        aug_k = jnp.dot(c3, sel_k.astype(BF16), preferred_element_type=F32) + ones_k
        qf = q_ref[0].astype(F32)
        kf = k_ref[0].astype(F32)
        vf = v_ref[0].astype(F32)
        if j == 1:
            qf = pltpu.roll(qf, hd, axis=1)
            kf = pltpu.roll(kf, hd, axis=1)
            vf = pltpu.roll(vf, hd, axis=1)
        qa_ref[j] = jnp.where(lane < hd, qf, aug_q).astype(BF16)
        ka_ref[j] = jnp.where(lane < hd, kf, aug_k).astype(BF16)
        vt_ref[j] = jnp.where(lane < hd, vf, ones_v).T.astype(BF16)

    krow = lax.broadcasted_iota(jnp.int32, (bk, bq), 0)
    qcol = lax.broadcasted_iota(jnp.int32, (bk, bq), 1)
    causal = krow <= qcol

    def tile(j, qblk, kj, carry, masked):
        m, acc = carry
        kblk = ka_ref[j, pl.ds(pl.multiple_of(kj * bk, bk), bk), :]
        s = lax.dot_general(kblk, qblk, nt, preferred_element_type=F32)
        if masked:
            s = jnp.where(causal, s, NEG_BIG)
        m_new = jnp.maximum(m, jnp.max(s, axis=0, keepdims=True))
        alpha = jnp.exp(m - m_new)
        p = jnp.exp(s - m_new).astype(BF16)
        vt = vt_ref[j, :, pl.ds(pl.multiple_of(kj * bk, bk), bk)]
        acc = acc * alpha + jnp.dot(vt, p, preferred_element_type=F32)
        return m_new, acc

    for j in range(2):
        def q_step(qi, _, j=j):
            q0 = pl.multiple_of(qi * bq, bq)
            qblk = qa_ref[j, pl.ds(q0, bq), :]
            carry = (jnp.full((1, bq), NEG_BIG, F32), jnp.zeros((LANES, bq), F32))
            carry = lax.fori_loop(0, qi, lambda kj, cr: tile(j, qblk, kj, cr, False), carry)
            m, acc = tile(j, qblk, qi, carry, True)
            res = acc.T
            out = res[:, 0:hd] / res[:, _ONE_LANE_V:_ONE_LANE_V + 1]
            o_ref[0, pl.ds(q0, bq), hd * j:hd * (j + 1)] = out.astype(BF16)
            return 0
        lax.fori_loop(0, seq // bq, q_step, 0)


def _fox(bq_arr, bk_arr, bv_arr, cum):
    bsz, seq, width = bq_arr.shape
    pairs = width // LANES
    bq = min(BQ_ATT, seq)
    bk = min(BK_ATT, seq)
    assert bq == bk
    hp = pl.BlockSpec((1, seq, LANES), lambda b, p: (b, 0, p))
    return pl.pallas_call(
        functools.partial(_fox_body, bq=bq, bk=bk),
        grid=(bsz, pairs),
        in_specs=[hp, hp, hp, pl.BlockSpec((1, seq, LANES), lambda b, p: (b, 0, 0))],
        out_specs=hp,
        out_shape=jax.ShapeDtypeStruct((bsz, seq, width), BF16),
        scratch_shapes=[pltpu.VMEM((2, seq, LANES), BF16),
                        pltpu.VMEM((2, seq, LANES), BF16),
                        pltpu.VMEM((2, LANES, seq), BF16)],
        compiler_params=_cparams(2),
        name="fox_attn",
    )(bq_arr, bk_arr, bv_arr, cum)


def _merge_body(x_ref, ya_ref, yb_ref, gate_ref, mod_ref, wa_ref, wb_ref, wo_ref, lnw_ref, lnb_ref,
                o_ref, *, alpha):
    d = x_ref.shape[2]
    ma = jnp.dot(ya_ref[0], wa_ref[...], preferred_element_type=F32)
    mb = jnp.dot(yb_ref[0], wb_ref[...], preferred_element_type=F32)
    ga = gate_ref[0, :, 0:d].astype(F32)
    gb = gate_ref[0, :, d:2 * d].astype(F32)
    merged = (ga * ma + gb * mb).astype(BF16)
    y = jnp.dot(merged, wo_ref[...], preferred_element_type=F32)
    g1 = mod_ref[0, 2:3, :]
    z = alpha * x_ref[0] + (1.0 + g1) * y
    o_ref[0] = _layer_norm(z, lnw_ref[...], lnb_ref[...])


def _merge(x, ya, yb, gates, mod3, wa, wb, wo, lnw, lnb, *, alpha):
    bsz, seq, d = x.shape
    tm = min(TM_OUT, seq)
    tok = lambda w: pl.BlockSpec((1, tm, w), lambda b, s: (b, s, 0))
    return pl.pallas_call(
        functools.partial(_merge_body, alpha=alpha),
        grid=(bsz, seq // tm),
        in_specs=[tok(d), tok(ya.shape[2]), tok(yb.shape[2]), tok(gates.shape[2]),
                  pl.BlockSpec((1, N_MOD, d), lambda b, s: (b, 0, 0)),
                  _const_spec(wa.shape), _const_spec(wb.shape), _const_spec(wo.shape),
                  _const_spec(lnw.shape), _const_spec(lnb.shape)],
        out_specs=tok(d),
        out_shape=jax.ShapeDtypeStruct((bsz, seq, d), F32),
        compiler_params=_cparams(2),
        name="merge_out_ln",
    )(x, ya, yb, gates, mod3, wa, wb, wo, lnw, lnb)


def _ffn_chunks(d_ff):
    out, start = [], 0
    while start < d_ff:
        size = min(1536, d_ff - start)
        out.append((start, size))
        start += size
    return out


def _ffn_body(x_ref, mod_ref, wg_ref, wu_ref, wd_ref, lnw_ref, lnb_ref, o_ref, *, alpha, chunks):
    x = x_ref[0]
    sh = mod_ref[0, 3:4, :]
    sc = mod_ref[0, 4:5, :]
    g2 = mod_ref[0, 5:6, :]
    h = (x * (1.0 + sc) + sh).astype(BF16)
    y = jnp.zeros(x.shape, F32)
    for start, size in chunks:
        gate = jnp.dot(h, wg_ref[:, start:start + size], preferred_element_type=F32)
        up = jnp.dot(h, wu_ref[:, start:start + size], preferred_element_type=F32)
        act = (gate * jax.nn.sigmoid(gate) * up).astype(BF16)
        y = y + jnp.dot(act, wd_ref[start:start + size, :], preferred_element_type=F32)
    z = alpha * x + (1.0 + g2) * y
    o_ref[0] = _layer_norm(z, lnw_ref[...], lnb_ref[...])


def _ffn(x, mod3, wg, wu, wd, lnw, lnb, *, alpha):
    bsz, seq, d = x.shape
    tm = min(TM_FFN, seq)
    tok = pl.BlockSpec((1, tm, d), lambda b, s: (b, s, 0))
    body = functools.partial(_ffn_body, alpha=alpha, chunks=_ffn_chunks(wg.shape[1]))
    return pl.pallas_call(
        body,
        grid=(bsz, seq // tm),
        in_specs=[tok, pl.BlockSpec((1, N_MOD, d), lambda b, s: (b, 0, 0)),
                  _const_spec(wg.shape), _const_spec(wu.shape), _const_spec(wd.shape),
                  _const_spec(lnw.shape), _const_spec(lnb.shape)],
        out_specs=tok,
        out_shape=jax.ShapeDtypeStruct((bsz, seq, d), F32),
        compiler_params=_cparams(2),
        name="ffn_ln",
    )(x, mod3, wg, wu, wd, lnw, lnb)


def kernel(x, c, w_ada, b_ada, w_in, fox_f_bias, lb_logits, hgrn_norm_w, w_branch_a, w_branch_b,
           w_out, ln1_w, ln1_b, w_ffn_gate, w_ffn_up, w_ffn_down, ln2_w, ln2_b):
    depth = w_ada.shape[0]
    assert depth == 1 and lb_logits.shape[0] == 2
    bsz, seq, d = x.shape
    a_width = w_branch_a.shape[1]
    b_width = w_branch_b.shape[1]
    b_heads = fox_f_bias.shape[1]
    alpha = (2.0 * depth) ** 0.25
    assert a_width % A_HEAD_DIM == 0 and b_width == b_heads * B_HEAD_DIM and b_heads <= LANES

    o_b = 4 * a_width
    o_f = o_b + 3 * b_width
    o_g = o_f + b_heads
    w_in0 = w_in[0]
    wa = w_in0[:, 0:o_b].astype(BF16)
    wb = w_in0[:, o_b:o_f].astype(BF16)
    wf = jnp.pad(w_in0[:, o_f:o_g], ((0, 0), (0, LANES - b_heads))).astype(BF16)
    wg = w_in0[:, o_g:].astype(BF16)
    fb = jnp.pad(fox_f_bias[0], (0, LANES - b_heads)).reshape(1, LANES)

    mod = _ada_mod(c, w_ada[0].astype(BF16), b_ada[0].reshape(1, -1))
    mod3 = mod.reshape(bsz, N_MOD, d)

    aq, ak, alf, av, aog, bq, bk, bv, cum, gates = _inproj(
        x, mod3, wa, wb, wf, wg, fb, lb_logits, a_width=a_width, b_width=b_width)

    ya = _hgrn(aq, ak, alf, av, aog, hgrn_norm_w[0].reshape(1, -1))
    yb = _fox(bq, bk, bv, cum)

    x1 = _merge(x, ya, yb, gates, mod3,
                w_branch_a[0].astype(BF16), w_branch_b[0].astype(BF16), w_out[0].astype(BF16),
                ln1_w[0].reshape(1, -1), ln1_b[0].reshape(1, -1), alpha=alpha)
    return _ffn(x1, mod3, w_ffn_gate[0].astype(BF16), w_ffn_up[0].astype(BF16),
                w_ffn_down[0].astype(BF16), ln2_w[0].reshape(1, -1), ln2_b[0].reshape(1, -1),
                alpha=alpha)
```

```python
import functools

import numpy as np
import jax
import jax.numpy as jnp
from jax import lax
from jax.experimental import pallas as pl
from jax.experimental.pallas import tpu as pltpu

F32 = jnp.float32
BF16 = jnp.bfloat16

A_HEAD_DIM = 128
B_HEAD_DIM = 64
CHUNK = 64
LN_EPS = 1e-5
RMS_EPS = 1e-6
N_MOD = 6

LANES = 128
VMEM_LIMIT_BYTES = 56 * 1024 * 1024

TS_INPROJ = 512
TT_HGRN = 256
HGRN_HALVES = (32, 16)
HGRN_LEAF = 16
BQ_ATT = 256
BK_ATT = 256
TM_OUT = 512
TM_FFN = 512
NEG_BIG = -1e30

_CQ_LANES = (64, 65, 66)
_CK_LANES = (67, 68, 69)
V_ROWS = 80


def _cparams(n_axes):
    return pltpu.CompilerParams(dimension_semantics=("arbitrary",) * n_axes,
                                vmem_limit_bytes=VMEM_LIMIT_BYTES)


def _const_spec(shape):
    nd = len(shape)
    return pl.BlockSpec(shape, lambda *_: (0,) * nd, pipeline_mode=pl.Buffered(1))


def _split3(x):
    hi = x.astype(BF16)
    r1 = x - hi.astype(F32)
    mid = r1.astype(BF16)
    lo = (r1 - mid.astype(F32)).astype(BF16)
    return hi, mid, lo


def _layer_norm(z, w, b):
    mu = jnp.mean(z, axis=-1, keepdims=True)
    zc = z - mu
    var = jnp.mean(zc * zc, axis=-1, keepdims=True)
    return zc * lax.rsqrt(var + LN_EPS) * w + b


def _ada_body(c_ref, w_ref, b_ref, o_ref):
    c = c_ref[...]
    act = c * jax.nn.sigmoid(c)
    o_ref[...] = jnp.dot(act.astype(BF16), w_ref[...], preferred_element_type=F32) + b_ref[...]


def _ada_mod(c, w_bf, b):
    bsz, d = c.shape
    n = w_bf.shape[1]
    tn = n // 4
    return pl.pallas_call(
        _ada_body,
        grid=(n // tn,),
        in_specs=[pl.BlockSpec((bsz, d), lambda j: (0, 0)),
                  pl.BlockSpec((d, tn), lambda j: (0, j)),
                  pl.BlockSpec((1, tn), lambda j: (0, j))],
        out_specs=pl.BlockSpec((bsz, tn), lambda j: (0, j)),
        out_shape=jax.ShapeDtypeStruct((bsz, n), F32),
        compiler_params=_cparams(1),
        name="ada_mod",
    )(c, w_bf, b)


def _fox_aug_selector(heads):
    sel = np.zeros((3 * LANES, 2 * heads * LANES), np.float32)
    one_row = LANES - 1
    for h in range(heads):
        qo = h * LANES
        ko = (heads + h) * LANES
        for part in range(3):
            sel[part * LANES + h, qo + _CQ_LANES[part]] = 1.0
            sel[one_row, qo + _CK_LANES[part]] = 1.0
            sel[one_row, ko + _CQ_LANES[part]] = 1.0
            sel[part * LANES + h, ko + _CK_LANES[part]] = -1.0
    return sel


def _inproj_body(x_ref, mod_ref, wa_ref, wb_ref, wf_ref, wg_ref, fb_ref, lbl_ref, sel_ref,
                 aq_ref, ak_ref, alf_ref, av_ref, aog_ref, qa_ref, ka_ref, vta_ref, gate_ref,
                 carry_ref, *, a_width, b_width, scale):
    @pl.when(pl.program_id(1) == 0)
    def _():
        carry_ref[...] = jnp.zeros_like(carry_ref)

    x = x_ref[0]
    sh = mod_ref[0, 0:1, :]
    sc = mod_ref[0, 1:2, :]
    h = (x * (1.0 + sc) + sh).astype(BF16)

    aw = a_width
    pa = jnp.dot(h, wa_ref[...], preferred_element_type=F32)
    l0 = lbl_ref[0:1, :]
    l1 = lbl_ref[1:2, :]
    lmax = jnp.maximum(l0, l1)
    e0 = jnp.exp(l0 - lmax)
    lb = e0 / (e0 + jnp.exp(l1 - lmax))
    f = lb + (1.0 - lb) * jax.nn.sigmoid(pa[:, aw:2 * aw])
    aq_ref[0] = pa[:, 0:aw].astype(BF16)
    ak_ref[0] = (1.0 - f).astype(BF16)
    alf_ref[0] = jnp.log(f)
    av_ref[0] = pa[:, 2 * aw:3 * aw].astype(BF16)
    aog_ref[0] = jax.nn.sigmoid(pa[:, 3 * aw:4 * aw]).astype(BF16)

    pf = jnp.dot(h, wf_ref[...], preferred_element_type=F32) + fb_ref[...]
    lsg = jnp.minimum(pf, 0.0) - jnp.log(1.0 + jnp.exp(-jnp.abs(pf)))
    ts = lsg.shape[0]
    r = lax.broadcasted_iota(jnp.int32, (ts, ts), 0)
    cidx = lax.broadcasted_iota(jnp.int32, (ts, ts), 1)
    tri = jnp.where(r >= cidx, 1.0, 0.0).astype(BF16)
    hi, mid, lo = _split3(lsg)
    parts = jnp.dot(tri, jnp.concatenate([hi, mid, lo], axis=1), preferred_element_type=F32)
    cum = parts[:, 0:LANES] + parts[:, LANES:2 * LANES] + parts[:, 2 * LANES:3 * LANES] + carry_ref[...]
    carry_ref[...] = cum[ts - 1:ts, :]

    bw = b_width
    heads = bw // B_HEAD_DIM
    pb = jnp.dot(h, wb_ref[...], preferred_element_type=F32)
    lane = lax.broadcasted_iota(jnp.int32, (ts, LANES), 1)
    cum1 = jnp.where(lane == LANES - 1, 1.0, cum)
    aug = jnp.dot(jnp.concatenate(_split3(cum1), axis=1), sel_ref[...], preferred_element_type=F32)
    for hh in range(heads):
        p, j = divmod(hh, LANES // B_HEAD_DIM)
        qx = pb[:, LANES * p:LANES * (p + 1)] * scale
        kx = pb[:, bw + LANES * p:bw + LANES * (p + 1)]
        if j:
            qx = pltpu.roll(qx, LANES - B_HEAD_DIM * j, axis=1)
            kx = pltpu.roll(kx, LANES - B_HEAD_DIM * j, axis=1)
        qa_ref[0, hh] = (jnp.where(lane < B_HEAD_DIM, qx, 0.0)
                         + aug[:, LANES * hh:LANES * (hh + 1)]).astype(BF16)
        ka_ref[0, hh] = (jnp.where(lane < B_HEAD_DIM, kx, 0.0)
                         + aug[:, LANES * (heads + hh):LANES * (heads + hh + 1)]).astype(BF16)
    vt = pb[:, 2 * bw:3 * bw].T
    pad_rows = vta_ref.shape[2] - B_HEAD_DIM
    one_row = jnp.where(lax.broadcasted_iota(jnp.int32, (pad_rows, ts), 0) == 0, 1.0, 0.0).astype(BF16)
    for hh in range(heads):
        vta_ref[0, hh, 0:B_HEAD_DIM, :] = vt[B_HEAD_DIM * hh:B_HEAD_DIM * (hh + 1), :].astype(BF16)
        vta_ref[0, hh, B_HEAD_DIM:B_HEAD_DIM + pad_rows, :] = one_row

    pg = jnp.dot(h, wg_ref[...], preferred_element_type=F32)
    gate_ref[0] = jax.nn.sigmoid(pg).astype(BF16)


def _inproj(x, mod3, wa, wb, wf, wg, fb, lbl, *, a_width, b_width):
    bsz, seq, d = x.shape
    ts = min(TS_INPROJ, seq)
    n_gate = wg.shape[1]
    heads = b_width // B_HEAD_DIM
    sel = jnp.asarray(_fox_aug_selector(heads), BF16)
    tok = lambda w: pl.BlockSpec((1, ts, w), lambda b, s: (b, s, 0))
    head_slot = pl.BlockSpec((1, heads, ts, LANES), lambda b, s: (b, 0, s, 0))
    out_shapes = [
        jax.ShapeDtypeStruct((bsz, seq, a_width), BF16),
        jax.ShapeDtypeStruct((bsz, seq, a_width), BF16),
        jax.ShapeDtypeStruct((bsz, seq, a_width), F32),
        jax.ShapeDtypeStruct((bsz, seq, a_width), BF16),
        jax.ShapeDtypeStruct((bsz, seq, a_width), BF16),
        jax.ShapeDtypeStruct((bsz, heads, seq, LANES), BF16),
        jax.ShapeDtypeStruct((bsz, heads, seq, LANES), BF16),
        jax.ShapeDtypeStruct((bsz, heads, V_ROWS, seq), BF16),
        jax.ShapeDtypeStruct((bsz, seq, n_gate), BF16),
    ]
    out_specs = [tok(a_width), tok(a_width), tok(a_width), tok(a_width), tok(a_width),
                 head_slot, head_slot,
                 pl.BlockSpec((1, heads, V_ROWS, ts), lambda b, s: (b, 0, 0, s)),
                 tok(n_gate)]
    body = functools.partial(_inproj_body, a_width=a_width, b_width=b_width,
                             scale=float(B_HEAD_DIM) ** -0.5)
    return pl.pallas_call(
        body,
        grid=(bsz, seq // ts),
        in_specs=[tok(d),
                  pl.BlockSpec((1, N_MOD, d), lambda b, s: (b, 0, 0)),
                  _const_spec(wa.shape), _const_spec(wb.shape), _const_spec(wf.shape),
                  _const_spec(wg.shape), _const_spec(fb.shape), _const_spec(lbl.shape),
                  _const_spec(sel.shape)],
        out_specs=out_specs,
        out_shape=out_shapes,
        scratch_shapes=[pltpu.VMEM((1, LANES), F32)],
        compiler_params=_cparams(2),
        name="in_proj",
    )(x, mod3, wa, wb, wf, wg, fb, lbl, sel)


def _hgrn_level_table(tt):
    t = np.arange(tt)[:, None]
    s = np.arange(tt)[None, :]
    lvl = np.full((tt, tt), -1, np.int32)
    same_chunk = (t // CHUNK) == (s // CHUNK)
    for i, m in enumerate(HGRN_HALVES):
        blk = 2 * m
        own = same_chunk & ((t // blk) == (s // blk)) & ((t % blk) >= m) & ((s % blk) < m)
        lvl[own] = i
    leaf = same_chunk & ((t // HGRN_LEAF) == (s // HGRN_LEAF)) & (t >= s)
    lvl[leaf] = len(HGRN_HALVES)
    return lvl


def _block_row(b, blk, r):
    t, l = b.shape
    x = b.reshape(t // blk, blk, l)
    g = x[:, r:r + 1, :]
    return jnp.broadcast_to(g, x.shape).reshape(t, l)


def _hgrn_body(q_ref, k_ref, lf_ref, v_ref, og_ref, nw_ref, lvl_ref, y_ref, st_ref, *, heads):
    @pl.when(pl.program_id(1) == 0)
    def _():
        st_ref[...] = jnp.zeros_like(st_ref)

    tt = q_ref.shape[1]
    hd = A_HEAD_DIM
    nch = tt // CHUNK
    lvl = lvl_ref[...]
    row = lax.broadcasted_iota(jnp.int32, (tt, hd), 0)

    r = lax.broadcasted_iota(jnp.int32, (tt, tt), 0)
    c = lax.broadcasted_iota(jnp.int32, (tt, tt), 1)
    tri = jnp.where((r >= c) & ((r // CHUNK) == (c // CHUNK)), 1.0, 0.0).astype(BF16)
    hi, mid, lo = _split3(lf_ref[0])
    b_all = (jnp.dot(tri, hi, preferred_element_type=F32)
             + jnp.dot(tri, mid, preferred_element_type=F32)
             + jnp.dot(tri, lo, preferred_element_type=F32))

    nt = (((1,), (1,)), ((), ()))
    for h in range(heads):
        sl = slice(hd * h, hd * (h + 1))
        q = q_ref[0, :, sl].astype(F32)
        k = k_ref[0, :, sl].astype(F32)
        v = v_ref[0, :, sl]
        b = b_all[:, sl]

        a = jnp.zeros((tt, tt), F32)
        for i, m in enumerate(HGRN_HALVES):
            g = _block_row(b, 2 * m, m - 1)
            later = (row % (2 * m)) >= m
            e = jnp.where(later, b - g, g - b)
            xm = jnp.where(later, q, k) * jnp.exp(e)
            qm = jnp.where(later, xm, 0.0).astype(BF16)
            km = jnp.where(later, 0.0, xm).astype(BF16)
            am = lax.dot_general(qm, km, nt, preferred_element_type=F32)
            a = jnp.where(lvl == i, am, a)
        g = _block_row(b, HGRN_LEAF, HGRN_LEAF // 2 - 1)
        e = b - g
        ql = (q * jnp.exp(e)).astype(BF16)
        kl = (k * jnp.exp(-e)).astype(BF16)
        al = lax.dot_general(ql, kl, nt, preferred_element_type=F32)
        a = jnp.where(lvl == len(HGRN_HALVES), al, a)
        o_intra = jnp.dot(a.astype(BF16), v, preferred_element_type=F32)

        st = st_ref[h]
        outs = []
        for ci in range(nch):
            rows = slice(CHUNK * ci, CHUNK * (ci + 1))
            bc = b[rows]
            b_last = bc[CHUNK - 1:CHUNK, :]
            qd = (q[rows] * jnp.exp(bc)).astype(BF16)
            o_inter = lax.dot_general(qd, st.astype(BF16), nt, preferred_element_type=F32)
            outs.append(o_inter + o_intra[rows])
            kd = (k[rows] * jnp.exp(b_last - bc)).astype(BF16)
            vt = v[rows].astype(F32).T.astype(BF16)
            st = st * jnp.exp(b_last) + jnp.dot(vt, kd, preferred_element_type=F32)
        st_ref[h] = st

        o = jnp.concatenate(outs, axis=0)
        ms = jnp.mean(o * o, axis=-1, keepdims=True)
        y = o * lax.rsqrt(ms + RMS_EPS) * nw_ref[:, sl] * og_ref[0, :, sl].astype(F32)
        y_ref[0, :, sl] = y.astype(BF16)


def _hgrn(aq, ak, alf, av, aog, norm_w):
    bsz, seq, width = aq.shape
    heads = width // A_HEAD_DIM
    tt = min(TT_HGRN, seq)
    lvl = jnp.asarray(_hgrn_level_table(tt))
    tok = pl.BlockSpec((1, tt, width), lambda b, s: (b, s, 0))
    return pl.pallas_call(
        functools.partial(_hgrn_body, heads=heads),
        grid=(bsz, seq // tt),
        in_specs=[tok, tok, tok, tok, tok, _const_spec((1, width)), _const_spec((tt, tt))],
        out_specs=tok,
        out_shape=jax.ShapeDtypeStruct((bsz, seq, width), BF16),
        scratch_shapes=[pltpu.VMEM((heads, A_HEAD_DIM, A_HEAD_DIM), F32)],
        compiler_params=_cparams(2),
        name="hgrn2",
    )(aq, ak, alf, av, aog, norm_w, lvl)


def _fox_body(qa_ref, ka_ref, vta_ref, o_ref, s_ref, p_ref, al_ref, m_ref, acc_ref, *, heads, bq, bk):
    qi = pl.program_id(1)
    nt = (((1,), (1,)), ((), ()))
    m_ref[...] = jnp.full(m_ref.shape, NEG_BIG, F32)
    acc_ref[...] = jnp.zeros(acc_ref.shape, F32)
    p_ref[...] = jnp.zeros(p_ref.shape, BF16)
    al_ref[...] = jnp.ones(al_ref.shape, F32)
    krow = lax.broadcasted_iota(jnp.int32, (bk, bq), 0)
    qcol = lax.broadcasted_iota(jnp.int32, (bk, bq), 1)
    causal = krow <= qcol

    def scores(kj):
        k0 = pl.multiple_of(kj * bk, bk)
        for h in range(heads):
            s_ref[h] = lax.dot_general(ka_ref[0, h, pl.ds(k0, bk), :], qa_ref[0, h], nt,
                                       preferred_element_type=F32)

    def numerators(masked):
        for h in range(heads):
            s = s_ref[h]
            if masked:
                s = jnp.where(causal, s, NEG_BIG)
            m_old = m_ref[h]
            m_new = jnp.maximum(m_old, jnp.max(s, axis=0, keepdims=True))
            al_ref[h] = jnp.exp(m_old - m_new)
            m_ref[h] = m_new
            p_ref[h] = jnp.exp(s - m_new).astype(BF16)

    def accumulate(kj):
        k0 = pl.multiple_of(kj * bk, bk)
        for h in range(heads):
            vt = vta_ref[0, h, :, pl.ds(k0, bk)]
            acc_ref[h] = acc_ref[h] * al_ref[h] + jnp.dot(vt, p_ref[h], preferred_element_type=F32)

    def off_diag(kj, carry):
        accumulate(jnp.maximum(kj - 1, 0))
        numerators(False)
        scores(kj + 1)
        return carry

    scores(0)
    lax.fori_loop(0, qi, off_diag, 0)
    accumulate(jnp.maximum(qi - 1, 0))
    numerators(True)
    accumulate(qi)

    outs = []
    for h in range(heads):
        a = acc_ref[h]
        outs.append(a[0:B_HEAD_DIM, :] / a[B_HEAD_DIM:B_HEAD_DIM + 1, :])
    o_ref[0] = jnp.concatenate(outs, axis=0).T.astype(BF16)


def _fox(qa, ka, vta):
    bsz, heads, seq, _ = qa.shape
    bq = min(BQ_ATT, seq)
    bk = min(BK_ATT, seq)
    assert bq == bk
    return pl.pallas_call(
        functools.partial(_fox_body, heads=heads, bq=bq, bk=bk),
        grid=(bsz, seq // bq),
        in_specs=[pl.BlockSpec((1, heads, bq, LANES), lambda b, i: (b, 0, i, 0)),
                  pl.BlockSpec((1, heads, seq, LANES), lambda b, i: (b, 0, 0, 0)),
                  pl.BlockSpec((1, heads, V_ROWS, seq), lambda b, i: (b, 0, 0, 0))],
        out_specs=pl.BlockSpec((1, bq, heads * B_HEAD_DIM), lambda b, i: (b, i, 0)),
        out_shape=jax.ShapeDtypeStruct((bsz, seq, heads * B_HEAD_DIM), BF16),
        scratch_shapes=[pltpu.VMEM((heads, bk, bq), F32),
                        pltpu.VMEM((heads, bk, bq), BF16),
                        pltpu.VMEM((heads, 1, bq), F32),
                        pltpu.VMEM((heads, 1, bq), F32),
                        pltpu.VMEM((heads, V_ROWS, bq), F32)],
        compiler_params=_cparams(2),
        name="fox_attn",
    )(qa, ka, vta)


def _merge_body(x_ref, ya_ref, yb_ref, gate_ref, mod_ref, wa_ref, wb_ref, wo_ref, lnw_ref, lnb_ref,
                o_ref, *, alpha):
    d = x_ref.shape[2]
    ma = jnp.dot(ya_ref[0], wa_ref[...], preferred_element_type=F32)
    mb = jnp.dot(yb_ref[0], wb_ref[...], preferred_element_type=F32)
    ga = gate_ref[0, :, 0:d].astype(F32)
    gb = gate_ref[0, :, d:2 * d].astype(F32)
    merged = (ga * ma + gb * mb).astype(BF16)
    y = jnp.dot(merged, wo_ref[...], preferred_element_type=F32)
    g1 = mod_ref[0, 2:3, :]
    z = alpha * x_ref[0] + (1.0 + g1) * y
    o_ref[0] = _layer_norm(z, lnw_ref[...], lnb_ref[...])


def _merge(x, ya, yb, gates, mod3, wa, wb, wo, lnw, lnb, *, alpha):
    bsz, seq, d = x.shape
    tm = min(TM_OUT, seq)
    tok = lambda w: pl.BlockSpec((1, tm, w), lambda b, s: (b, s, 0))
    return pl.pallas_call(
        functools.partial(_merge_body, alpha=alpha),
        grid=(bsz, seq // tm),
        in_specs=[tok(d), tok(ya.shape[2]), tok(yb.shape[2]), tok(gates.shape[2]),
                  pl.BlockSpec((1, N_MOD, d), lambda b, s: (b, 0, 0)),
                  _const_spec(wa.shape), _const_spec(wb.shape), _const_spec(wo.shape),
                  _const_spec(lnw.shape), _const_spec(lnb.shape)],
        out_specs=tok(d),
        out_shape=jax.ShapeDtypeStruct((bsz, seq, d), F32),
        compiler_params=_cparams(2),
        name="merge_out_ln",
    )(x, ya, yb, gates, mod3, wa, wb, wo, lnw, lnb)


def _ffn_chunks(d_ff):
    out, start = [], 0
    while start < d_ff:
        size = min(1536, d_ff - start)
        out.append((start, size))
        start += size
    return out


def _ffn_body(x_ref, mod_ref, wg_ref, wu_ref, wd_ref, lnw_ref, lnb_ref, o_ref, *, alpha, chunks):
    x = x_ref[0]
    sh = mod_ref[0, 3:4, :]
    sc = mod_ref[0, 4:5, :]
    g2 = mod_ref[0, 5:6, :]
    h = (x * (1.0 + sc) + sh).astype(BF16)
    y = jnp.zeros(x.shape, F32)
    for start, size in chunks:
        gate = jnp.dot(h, wg_ref[:, start:start + size], preferred_element_type=F32)
        up = jnp.dot(h, wu_ref[:, start:start + size], preferred_element_type=F32)
        act = (gate * jax.nn.sigmoid(gate) * up).astype(BF16)
        y = y + jnp.dot(act, wd_ref[start:start + size, :], preferred_element_type=F32)
    z = alpha * x + (1.0 + g2) * y
    o_ref[0] = _layer_norm(z, lnw_ref[...], lnb_ref[...])


def _ffn(x, mod3, wg, wu, wd, lnw, lnb, *, alpha):
    bsz, seq, d = x.shape
    tm = min(TM_FFN, seq)
    tok = pl.BlockSpec((1, tm, d), lambda b, s: (b, s, 0))
    body = functools.partial(_ffn_body, alpha=alpha, chunks=_ffn_chunks(wg.shape[1]))
    return pl.pallas_call(
        body,
        grid=(bsz, seq // tm),
        in_specs=[tok, pl.BlockSpec((1, N_MOD, d), lambda b, s: (b, 0, 0)),
                  _const_spec(wg.shape), _const_spec(wu.shape), _const_spec(wd.shape),
                  _const_spec(lnw.shape), _const_spec(lnb.shape)],
        out_specs=tok,
        out_shape=jax.ShapeDtypeStruct((bsz, seq, d), F32),
        compiler_params=_cparams(2),
        name="ffn_ln",
    )(x, mod3, wg, wu, wd, lnw, lnb)


def kernel(x, c, w_ada, b_ada, w_in, fox_f_bias, lb_logits, hgrn_norm_w, w_branch_a, w_branch_b,
           w_out, ln1_w, ln1_b, w_ffn_gate, w_ffn_up, w_ffn_down, ln2_w, ln2_b):
    depth = w_ada.shape[0]
    assert depth == 1 and lb_logits.shape[0] == 2
    bsz, seq, d = x.shape
    a_width = w_branch_a.shape[1]
    b_width = w_branch_b.shape[1]
    b_heads = fox_f_bias.shape[1]
    alpha = (2.0 * depth) ** 0.25
    assert a_width % A_HEAD_DIM == 0 and b_width == b_heads * B_HEAD_DIM and b_heads <= LANES

    o_b = 4 * a_width
    o_f = o_b + 3 * b_width
    o_g = o_f + b_heads
    w_in0 = w_in[0]
    wa = w_in0[:, 0:o_b].astype(BF16)
    wb = w_in0[:, o_b:o_f].astype(BF16)
    wf = jnp.pad(w_in0[:, o_f:o_g], ((0, 0), (0, LANES - b_heads))).astype(BF16)
    wg = w_in0[:, o_g:].astype(BF16)
    fb = jnp.pad(fox_f_bias[0], (0, LANES - b_heads)).reshape(1, LANES)

    mod = _ada_mod(c, w_ada[0].astype(BF16), b_ada[0].reshape(1, -1))
    mod3 = mod.reshape(bsz, N_MOD, d)

    aq, ak, alf, av, aog, qa, ka, vta, gates = _inproj(
        x, mod3, wa, wb, wf, wg, fb, lb_logits, a_width=a_width, b_width=b_width)

    ya = _hgrn(aq, ak, alf, av, aog, hgrn_norm_w[0].reshape(1, -1))
    yb = _fox(qa, ka, vta)

    x1 = _merge(x, ya, yb, gates, mod3,
                w_branch_a[0].astype(BF16), w_branch_b[0].astype(BF16), w_out[0].astype(BF16),
                ln1_w[0].reshape(1, -1), ln1_b[0].reshape(1, -1), alpha=alpha)
    return _ffn(x1, mod3, w_ffn_gate[0].astype(BF16), w_ffn_up[0].astype(BF16),
                w_ffn_down[0].astype(BF16), ln2_w[0].reshape(1, -1), ln2_b[0].reshape(1, -1),
                alpha=alpha)
```

```python
import functools

import numpy as np
import jax
import jax.numpy as jnp
from jax import lax
from jax.experimental import pallas as pl
from jax.experimental.pallas import tpu as pltpu

F32 = jnp.float32
BF16 = jnp.bfloat16

A_HEAD_DIM = 128
B_HEAD_DIM = 64
CHUNK = 64
LN_EPS = 1e-5
RMS_EPS = 1e-6
N_MOD = 6

LANES = 128
VMEM_LIMIT_BYTES = 56 * 1024 * 1024

TS_INPROJ = 512
TT_HGRN = 256
HGRN_HALVES = (32, 16)
HGRN_LEAF = 16
BQ_ATT = 512
BK_ATT = 256
TM_OUT = 512
TM_FFN = 512
NEG_BIG = -1e30

_CQ_LANES = (64, 65, 66)
_CK_LANES = (67, 68, 69)
V_ROWS = 80
_AUG_GROUP = 16
LOG2E = 1.4426950408889634


def _cparams(n_axes):
    return pltpu.CompilerParams(dimension_semantics=("arbitrary",) * n_axes,
                                vmem_limit_bytes=VMEM_LIMIT_BYTES)


def _const_spec(shape):
    nd = len(shape)
    return pl.BlockSpec(shape, lambda *_: (0,) * nd, pipeline_mode=pl.Buffered(1))


def _split3(x):
    hi = x.astype(BF16)
    r1 = x - hi.astype(F32)
    mid = r1.astype(BF16)
    lo = (r1 - mid.astype(F32)).astype(BF16)
    return hi, mid, lo


def _layer_norm(z, w, b):
    mu = jnp.mean(z, axis=-1, keepdims=True)
    zc = z - mu
    var = jnp.mean(zc * zc, axis=-1, keepdims=True)
    return zc * lax.rsqrt(var + LN_EPS) * w + b


def _ada_body(c_ref, w_ref, b_ref, o_ref):
    c = c_ref[...]
    act = c * jax.nn.sigmoid(c)
    o_ref[...] = jnp.dot(act.astype(BF16), w_ref[...], preferred_element_type=F32) + b_ref[...]


def _ada_mod(c, w_bf, b):
    bsz, d = c.shape
    n = w_bf.shape[1]
    tn = n // 4
    return pl.pallas_call(
        _ada_body,
        grid=(n // tn,),
        in_specs=[pl.BlockSpec((bsz, d), lambda j: (0, 0)),
                  pl.BlockSpec((d, tn), lambda j: (0, j)),
                  pl.BlockSpec((1, tn), lambda j: (0, j))],
        out_specs=pl.BlockSpec((bsz, tn), lambda j: (0, j)),
        out_shape=jax.ShapeDtypeStruct((bsz, n), F32),
        compiler_params=_cparams(1),
        name="ada_mod",
    )(c, w_bf, b)


def _fox_aug_selector(heads):
    assert _AUG_GROUP * heads <= LANES and 3 * heads < LANES
    sel = np.zeros((LANES, 2 * LANES), np.float32)
    one_row = LANES - 1
    for h in range(heads):
        for part in range(3):
            sel[part * heads + h, _AUG_GROUP * h + part] = 1.0
            sel[one_row, _AUG_GROUP * h + 3 + part] = 1.0
            sel[one_row, LANES + _AUG_GROUP * h + part] = 1.0
            sel[part * heads + h, LANES + _AUG_GROUP * h + 3 + part] = -1.0
    return sel


def _inproj_body(x_ref, mod_ref, wa_ref, wb_ref, wf_ref, wg_ref, fb_ref, lbl_ref, sel_ref,
                 aq_ref, ak_ref, alf_ref, av_ref, aog_ref, qa_ref, ka_ref, vta_ref, gate_ref,
                 carry_ref, *, a_width, b_width, scale):
    @pl.when(pl.program_id(1) == 0)
    def _():
        carry_ref[...] = jnp.zeros_like(carry_ref)

    x = x_ref[0]
    sh = mod_ref[0, 0:1, :]
    sc = mod_ref[0, 1:2, :]
    h = (x * (1.0 + sc) + sh).astype(BF16)

    aw = a_width
    pa = jnp.dot(h, wa_ref[...], preferred_element_type=F32)
    l0 = lbl_ref[0:1, :]
    l1 = lbl_ref[1:2, :]
    lmax = jnp.maximum(l0, l1)
    e0 = jnp.exp(l0 - lmax)
    lb = e0 / (e0 + jnp.exp(l1 - lmax))
    f = lb + (1.0 - lb) * jax.nn.sigmoid(pa[:, aw:2 * aw])
    aq_ref[0] = pa[:, 0:aw].astype(BF16)
    ak_ref[0] = (1.0 - f).astype(BF16)
    alf_ref[0] = jnp.log(f)
    av_ref[0] = pa[:, 2 * aw:3 * aw].astype(BF16)
    aog_ref[0] = jax.nn.sigmoid(pa[:, 3 * aw:4 * aw]).astype(BF16)

    pf = jnp.dot(h, wf_ref[...], preferred_element_type=F32) + fb_ref[...]
    lsg = jnp.minimum(pf, 0.0) - jnp.log(1.0 + jnp.exp(-jnp.abs(pf)))
    ts = lsg.shape[0]
    r = lax.broadcasted_iota(jnp.int32, (ts, ts), 0)
    cidx = lax.broadcasted_iota(jnp.int32, (ts, ts), 1)
    tri = jnp.where(r >= cidx, 1.0, 0.0).astype(BF16)
    hi, mid, lo = _split3(lsg)
    parts = jnp.dot(tri, jnp.concatenate([hi, mid, lo], axis=1), preferred_element_type=F32)
    cum = parts[:, 0:LANES] + parts[:, LANES:2 * LANES] + parts[:, 2 * LANES:3 * LANES] + carry_ref[...]
    carry_ref[...] = cum[ts - 1:ts, :]

    bw = b_width
    heads = bw // B_HEAD_DIM
    pb = jnp.dot(h, wb_ref[...], preferred_element_type=F32)
    lane = lax.broadcasted_iota(jnp.int32, (ts, LANES), 1)
    c_hi, c_mid, c_lo = _split3(cum * LOG2E)
    comp = jnp.where(lane < heads, c_hi.astype(F32),
                     jnp.where(lane < 2 * heads, pltpu.roll(c_mid.astype(F32), heads, axis=1),
                               jnp.where(lane < 3 * heads, pltpu.roll(c_lo.astype(F32), 2 * heads, axis=1),
                                         jnp.where(lane == LANES - 1, 1.0, 0.0))))
    groups = jnp.dot(comp.astype(BF16), sel_ref[...], preferred_element_type=F32)
    aug_lo, aug_hi = _CQ_LANES[0], _CK_LANES[2] + 1
    for hh in range(heads):
        p, j = divmod(hh, LANES // B_HEAD_DIM)
        qx = pb[:, LANES * p:LANES * (p + 1)] * (scale * LOG2E)
        kx = pb[:, bw + LANES * p:bw + LANES * (p + 1)]
        if j:
            qx = pltpu.roll(qx, LANES - B_HEAD_DIM * j, axis=1)
            kx = pltpu.roll(kx, LANES - B_HEAD_DIM * j, axis=1)
        shift = (aug_lo - _AUG_GROUP * hh) % LANES
        gq = groups[:, 0:LANES]
        gk = groups[:, LANES:2 * LANES]
        if shift:
            gq = pltpu.roll(gq, shift, axis=1)
            gk = pltpu.roll(gk, shift, axis=1)
        qa_ref[0, hh] = jnp.where(lane < aug_lo, qx, jnp.where(lane < aug_hi, gq, 0.0)).astype(BF16)
        ka_ref[0, hh] = jnp.where(lane < aug_lo, kx, jnp.where(lane < aug_hi, gk, 0.0)).astype(BF16)
    vt = pb[:, 2 * bw:3 * bw].T
    pad_rows = vta_ref.shape[2] - B_HEAD_DIM
    one_row = jnp.where(lax.broadcasted_iota(jnp.int32, (pad_rows, ts), 0) == 0, 1.0, 0.0).astype(BF16)
    for hh in range(heads):
        vta_ref[0, hh, 0:B_HEAD_DIM, :] = vt[B_HEAD_DIM * hh:B_HEAD_DIM * (hh + 1), :].astype(BF16)
        vta_ref[0, hh, B_HEAD_DIM:B_HEAD_DIM + pad_rows, :] = one_row

    pg = jnp.dot(h, wg_ref[...], preferred_element_type=F32)
    gate_ref[0] = jax.nn.sigmoid(pg).astype(BF16)


def _inproj(x, mod3, wa, wb, wf, wg, fb, lbl, *, a_width, b_width):
    bsz, seq, d = x.shape
    ts = min(TS_INPROJ, seq)
    n_gate = wg.shape[1]
    heads = b_width // B_HEAD_DIM
    sel = jnp.asarray(_fox_aug_selector(heads), BF16)
    tok = lambda w: pl.BlockSpec((1, ts, w), lambda b, s: (b, s, 0))
    head_slot = pl.BlockSpec((1, heads, ts, LANES), lambda b, s: (b, 0, s, 0))
    out_shapes = [
        jax.ShapeDtypeStruct((bsz, seq, a_width), BF16),
        jax.ShapeDtypeStruct((bsz, seq, a_width), BF16),
        jax.ShapeDtypeStruct((bsz, seq, a_width), F32),
        jax.ShapeDtypeStruct((bsz, seq, a_width), BF16),
        jax.ShapeDtypeStruct((bsz, seq, a_width), BF16),
        jax.ShapeDtypeStruct((bsz, heads, seq, LANES), BF16),
        jax.ShapeDtypeStruct((bsz, heads, seq, LANES), BF16),
        jax.ShapeDtypeStruct((bsz, heads, V_ROWS, seq), BF16),
        jax.ShapeDtypeStruct((bsz, seq, n_gate), BF16),
    ]
    out_specs = [tok(a_width), tok(a_width), tok(a_width), tok(a_width), tok(a_width),
                 head_slot, head_slot,
                 pl.BlockSpec((1, heads, V_ROWS, ts), lambda b, s: (b, 0, 0, s)),
                 tok(n_gate)]
    body = functools.partial(_inproj_body, a_width=a_width, b_width=b_width,
                             scale=float(B_HEAD_DIM) ** -0.5)
    return pl.pallas_call(
        body,
        grid=(bsz, seq // ts),
        in_specs=[tok(d),
                  pl.BlockSpec((1, N_MOD, d), lambda b, s: (b, 0, 0)),
                  _const_spec(wa.shape), _const_spec(wb.shape), _const_spec(wf.shape),
                  _const_spec(wg.shape), _const_spec(fb.shape), _const_spec(lbl.shape),
                  _const_spec(sel.shape)],
        out_specs=out_specs,
        out_shape=out_shapes,
        scratch_shapes=[pltpu.VMEM((1, LANES), F32)],
        compiler_params=_cparams(2),
        name="in_proj",
    )(x, mod3, wa, wb, wf, wg, fb, lbl, sel)


def _hgrn_level_table(tt):
    t = np.arange(tt)[:, None]
    s = np.arange(tt)[None, :]
    lvl = np.full((tt, tt), -1, np.int32)
    same_chunk = (t // CHUNK) == (s // CHUNK)
    for i, m in enumerate(HGRN_HALVES):
        blk = 2 * m
        own = same_chunk & ((t // blk) == (s // blk)) & ((t % blk) >= m) & ((s % blk) < m)
        lvl[own] = i
    leaf = same_chunk & ((t // HGRN_LEAF) == (s // HGRN_LEAF)) & (t >= s)
    lvl[leaf] = len(HGRN_HALVES)
    return lvl


def _block_row(b, blk, r):
    t, l = b.shape
    x = b.reshape(t // blk, blk, l)
    g = x[:, r:r + 1, :]
    return jnp.broadcast_to(g, x.shape).reshape(t, l)


def _hgrn_body(q_ref, k_ref, lf_ref, v_ref, og_ref, nw_ref, lvl_ref, y_ref, st_ref, *, heads):
    @pl.when(pl.program_id(1) == 0)
    def _():
        st_ref[...] = jnp.zeros_like(st_ref)

    tt = q_ref.shape[1]
    hd = A_HEAD_DIM
    nch = tt // CHUNK
    lvl = lvl_ref[...]
    owned = [lvl == i for i in range(len(HGRN_HALVES) + 1)]
    row = lax.broadcasted_iota(jnp.int32, (tt, hd), 0)
    later_half = [(row % (2 * m)) >= m for m in HGRN_HALVES]

    r = lax.broadcasted_iota(jnp.int32, (tt, tt), 0)
    c = lax.broadcasted_iota(jnp.int32, (tt, tt), 1)
    tri = jnp.where((r >= c) & ((r // CHUNK) == (c // CHUNK)), 1.0, 0.0).astype(BF16)
    hi, mid, lo = _split3(lf_ref[0])
    b_all = (jnp.dot(tri, hi, preferred_element_type=F32)
             + jnp.dot(tri, mid, preferred_element_type=F32)
             + jnp.dot(tri, lo, preferred_element_type=F32))

    nt = (((1,), (1,)), ((), ()))
    for h in range(heads):
        sl = slice(hd * h, hd * (h + 1))
        q = q_ref[0, :, sl].astype(F32)
        k = k_ref[0, :, sl].astype(F32)
        v = v_ref[0, :, sl]
        b = b_all[:, sl]

        a = jnp.zeros((tt, tt), F32)
        for i, m in enumerate(HGRN_HALVES):
            g = _block_row(b, 2 * m, m - 1)
            later = later_half[i]
            e = jnp.where(later, b - g, g - b)
            xm = jnp.where(later, q, k) * jnp.exp(e)
            qm = jnp.where(later, xm, 0.0).astype(BF16)
            km = jnp.where(later, 0.0, xm).astype(BF16)
            am = lax.dot_general(qm, km, nt, preferred_element_type=F32)
            a = jnp.where(owned[i], am, a)
        g = _block_row(b, HGRN_LEAF, HGRN_LEAF // 2 - 1)
        e = b - g
        ql = (q * jnp.exp(e)).astype(BF16)
        kl = (k * jnp.exp(-e)).astype(BF16)
        al = lax.dot_general(ql, kl, nt, preferred_element_type=F32)
        a = jnp.where(owned[-1], al, a)
        o_intra = jnp.dot(a.astype(BF16), v, preferred_element_type=F32)

        st = st_ref[h]
        outs = []
        for ci in range(nch):
            rows = slice(CHUNK * ci, CHUNK * (ci + 1))
            bc = b[rows]
            b_last = bc[CHUNK - 1:CHUNK, :]
            qd = (q[rows] * jnp.exp(bc)).astype(BF16)
            o_inter = lax.dot_general(qd, st.astype(BF16), nt, preferred_element_type=F32)
            outs.append(o_inter + o_intra[rows])
            kd = (k[rows] * jnp.exp(b_last - bc)).astype(BF16)
            vt = v[rows].astype(F32).T.astype(BF16)
            st = st * jnp.exp(b_last) + jnp.dot(vt, kd, preferred_element_type=F32)
        st_ref[h] = st

        o = jnp.concatenate(outs, axis=0)
        ms = jnp.mean(o * o, axis=-1, keepdims=True)
        y = o * lax.rsqrt(ms + RMS_EPS) * nw_ref[:, sl] * og_ref[0, :, sl].astype(F32)
        y_ref[0, :, sl] = y.astype(BF16)


def _hgrn(aq, ak, alf, av, aog, norm_w):
    bsz, seq, width = aq.shape
    heads = width // A_HEAD_DIM
    tt = min(TT_HGRN, seq)
    lvl = jnp.asarray(_hgrn_level_table(tt))
    tok = pl.BlockSpec((1, tt, width), lambda b, s: (b, s, 0))
    return pl.pallas_call(
        functools.partial(_hgrn_body, heads=heads),
        grid=(bsz, seq // tt),
        in_specs=[tok, tok, tok, tok, tok, _const_spec((1, width)), _const_spec((tt, tt))],
        out_specs=tok,
        out_shape=jax.ShapeDtypeStruct((bsz, seq, width), BF16),
        scratch_shapes=[pltpu.VMEM((heads, A_HEAD_DIM, A_HEAD_DIM), F32)],
        compiler_params=_cparams(2),
        name="hgrn2",
    )(aq, ak, alf, av, aog, norm_w, lvl)


def _fox_body(qa_ref, ka_ref, vta_ref, o_ref, s_ref, tmax_ref, m_ref, acc_ref, *, heads, bq, bk):
    qi = pl.program_id(1)
    r = bq // bk
    nt = (((1,), (1,)), ((), ()))
    m_ref[...] = jnp.full(m_ref.shape, NEG_BIG, F32)
    acc_ref[...] = jnp.zeros(acc_ref.shape, F32)

    def scores(kj, slot, h, visible=None):
        k0 = pl.multiple_of(kj * bk, bk)
        s = lax.dot_general(ka_ref[0, h, pl.ds(k0, bk), :], qa_ref[0, h], nt,
                            preferred_element_type=F32)
        if visible is not None:
            s = jnp.where(visible, s, NEG_BIG)
        s_ref[slot, h] = s
        tmax_ref[slot, h] = jnp.max(s, axis=0, keepdims=True)

    def softmax_accumulate(kj, slot, h, visible=None):
        s = s_ref[slot, h]
        if visible is None:
            tmax = tmax_ref[slot, h]
        else:
            s = jnp.where(visible, s, NEG_BIG)
            tmax = jnp.max(s, axis=0, keepdims=True)
        m_old = m_ref[h]
        m_new = jnp.maximum(m_old, tmax)
        alpha = jnp.exp2(m_old - m_new)
        m_ref[h] = m_new
        p = jnp.exp2(s - m_new).astype(BF16)
        k0 = pl.multiple_of(kj * bk, bk)
        vt = vta_ref[0, h, :, pl.ds(k0, bk)]
        acc_ref[h] = acc_ref[h] * alpha + jnp.dot(vt, p, preferred_element_type=F32)

    def trip(t, carry):
        for i in range(r):
            kj = t * r + i
            for h in range(heads):
                scores(kj + 1, (i + 1) % r, h)
                softmax_accumulate(kj, i, h)
        return carry

    for h in range(heads):
        scores(0, 0, h)
    lax.fori_loop(0, qi, trip, 0)

    krow = lax.broadcasted_iota(jnp.int32, (bk, bq), 0)
    qcol = lax.broadcasted_iota(jnp.int32, (bk, bq), 1)
    for i in range(r):
        kj = qi * r + i
        for h in range(heads):
            if i + 1 < r:
                scores(kj + 1, i + 1, h, visible=(krow + (i + 1) * bk <= qcol))
            softmax_accumulate(kj, i, h, visible=(krow <= qcol) if i == 0 else None)

    outs = []
    for h in range(heads):
        a = acc_ref[h]
        outs.append(a[0:B_HEAD_DIM, :] / a[B_HEAD_DIM:B_HEAD_DIM + 1, :])
    o_ref[0] = jnp.concatenate(outs, axis=0).T.astype(BF16)


def _fox(qa, ka, vta):
    bsz, heads, seq, _ = qa.shape
    bq = min(BQ_ATT, seq)
    bk = min(BK_ATT, seq)
    assert bq % bk == 0 and seq % bq == 0
    return pl.pallas_call(
        functools.partial(_fox_body, heads=heads, bq=bq, bk=bk),
        grid=(bsz, seq // bq),
        in_specs=[pl.BlockSpec((1, heads, bq, LANES), lambda b, i: (b, 0, i, 0)),
                  pl.BlockSpec((1, heads, seq, LANES), lambda b, i: (b, 0, 0, 0)),
                  pl.BlockSpec((1, heads, V_ROWS, seq), lambda b, i: (b, 0, 0, 0))],
        out_specs=pl.BlockSpec((1, bq, heads * B_HEAD_DIM), lambda b, i: (b, i, 0)),
        out_shape=jax.ShapeDtypeStruct((bsz, seq, heads * B_HEAD_DIM), BF16),
        scratch_shapes=[pltpu.VMEM((bq // bk, heads, bk, bq), F32),
                        pltpu.VMEM((bq // bk, heads, 1, bq), F32),
                        pltpu.VMEM((heads, 1, bq), F32),
                        pltpu.VMEM((heads, V_ROWS, bq), F32)],
        compiler_params=_cparams(2),
        name="fox_attn",
    )(qa, ka, vta)


def _merge_body(x_ref, ya_ref, yb_ref, gate_ref, mod_ref, wa_ref, wb_ref, wo_ref, lnw_ref, lnb_ref,
                o_ref, *, alpha):
    d = x_ref.shape[2]
    ma = jnp.dot(ya_ref[0], wa_ref[...], preferred_element_type=F32)
    mb = jnp.dot(yb_ref[0], wb_ref[...], preferred_element_type=F32)
    ga = gate_ref[0, :, 0:d].astype(F32)
    gb = gate_ref[0, :, d:2 * d].astype(F32)
    merged = (ga * ma + gb * mb).astype(BF16)
    y = jnp.dot(merged, wo_ref[...], preferred_element_type=F32)
    g1 = mod_ref[0, 2:3, :]
    z = alpha * x_ref[0] + (1.0 + g1) * y
    o_ref[0] = _layer_norm(z, lnw_ref[...], lnb_ref[...])


def _merge(x, ya, yb, gates, mod3, wa, wb, wo, lnw, lnb, *, alpha):
    bsz, seq, d = x.shape
    tm = min(TM_OUT, seq)
    tok = lambda w: pl.BlockSpec((1, tm, w), lambda b, s: (b, s, 0))
    return pl.pallas_call(
        functools.partial(_merge_body, alpha=alpha),
        grid=(bsz, seq // tm),
        in_specs=[tok(d), tok(ya.shape[2]), tok(yb.shape[2]), tok(gates.shape[2]),
                  pl.BlockSpec((1, N_MOD, d), lambda b, s: (b, 0, 0)),
                  _const_spec(wa.shape), _const_spec(wb.shape), _const_spec(wo.shape),
                  _const_spec(lnw.shape), _const_spec(lnb.shape)],
        out_specs=tok(d),
        out_shape=jax.ShapeDtypeStruct((bsz, seq, d), F32),
        compiler_params=_cparams(2),
        name="merge_out_ln",
    )(x, ya, yb, gates, mod3, wa, wb, wo, lnw, lnb)


def _ffn_chunks(d_ff):
    out, start = [], 0
    while start < d_ff:
        size = min(1536, d_ff - start)
        out.append((start, size))
        start += size
    return out


def _ffn_body(x_ref, mod_ref, wg_ref, wu_ref, wd_ref, lnw_ref, lnb_ref, o_ref, *, alpha, chunks):
    x = x_ref[0]
    sh = mod_ref[0, 3:4, :]
    sc = mod_ref[0, 4:5, :]
    g2 = mod_ref[0, 5:6, :]
    h = (x * (1.0 + sc) + sh).astype(BF16)
    y = jnp.zeros(x.shape, F32)
    for start, size in chunks:
        gate = jnp.dot(h, wg_ref[:, start:start + size], preferred_element_type=F32)
        up = jnp.dot(h, wu_ref[:, start:start + size], preferred_element_type=F32)
        act = (gate * jax.nn.sigmoid(gate) * up).astype(BF16)
        y = y + jnp.dot(act, wd_ref[start:start + size, :], preferred_element_type=F32)
    z = alpha * x + (1.0 + g2) * y
    o_ref[0] = _layer_norm(z, lnw_ref[...], lnb_ref[...])


def _ffn(x, mod3, wg, wu, wd, lnw, lnb, *, alpha):
    bsz, seq, d = x.shape
    tm = min(TM_FFN, seq)
    tok = pl.BlockSpec((1, tm, d), lambda b, s: (b, s, 0))
    body = functools.partial(_ffn_body, alpha=alpha, chunks=_ffn_chunks(wg.shape[1]))
    return pl.pallas_call(
        body,
        grid=(bsz, seq // tm),
        in_specs=[tok, pl.BlockSpec((1, N_MOD, d), lambda b, s: (b, 0, 0)),
                  _const_spec(wg.shape), _const_spec(wu.shape), _const_spec(wd.shape),
                  _const_spec(lnw.shape), _const_spec(lnb.shape)],
        out_specs=tok,
        out_shape=jax.ShapeDtypeStruct((bsz, seq, d), F32),
        compiler_params=_cparams(2),
        name="ffn_ln",
    )(x, mod3, wg, wu, wd, lnw, lnb)


def kernel(x, c, w_ada, b_ada, w_in, fox_f_bias, lb_logits, hgrn_norm_w, w_branch_a, w_branch_b,
           w_out, ln1_w, ln1_b, w_ffn_gate, w_ffn_up, w_ffn_down, ln2_w, ln2_b):
    depth = w_ada.shape[0]
    assert depth == 1 and lb_logits.shape[0] == 2
    bsz, seq, d = x.shape
    a_width = w_branch_a.shape[1]
    b_width = w_branch_b.shape[1]
    b_heads = fox_f_bias.shape[1]
    alpha = (2.0 * depth) ** 0.25
    assert a_width % A_HEAD_DIM == 0 and b_width == b_heads * B_HEAD_DIM and b_heads <= LANES

    o_b = 4 * a_width
    o_f = o_b + 3 * b_width
    o_g = o_f + b_heads
    w_in0 = w_in[0]
    wa = w_in0[:, 0:o_b].astype(BF16)
    wb = w_in0[:, o_b:o_f].astype(BF16)
    wf = jnp.pad(w_in0[:, o_f:o_g], ((0, 0), (0, LANES - b_heads))).astype(BF16)
    wg = w_in0[:, o_g:].astype(BF16)
    fb = jnp.pad(fox_f_bias[0], (0, LANES - b_heads)).reshape(1, LANES)

    mod = _ada_mod(c, w_ada[0].astype(BF16), b_ada[0].reshape(1, -1))
    mod3 = mod.reshape(bsz, N_MOD, d)

    aq, ak, alf, av, aog, qa, ka, vta, gates = _inproj(
        x, mod3, wa, wb, wf, wg, fb, lb_logits, a_width=a_width, b_width=b_width)

    ya = _hgrn(aq, ak, alf, av, aog, hgrn_norm_w[0].reshape(1, -1))
    yb = _fox(qa, ka, vta)

    x1 = _merge(x, ya, yb, gates, mod3,
                w_branch_a[0].astype(BF16), w_branch_b[0].astype(BF16), w_out[0].astype(BF16),
                ln1_w[0].reshape(1, -1), ln1_b[0].reshape(1, -1), alpha=alpha)
    return _ffn(x1, mod3, w_ffn_gate[0].astype(BF16), w_ffn_up[0].astype(BF16),
                w_ffn_down[0].astype(BF16), ln2_w[0].reshape(1, -1), ln2_b[0].reshape(1, -1),
                alpha=alpha)
```

```python
import functools

import numpy as np
import jax
import jax.numpy as jnp
from jax import lax
from jax.experimental import pallas as pl
from jax.experimental.pallas import tpu as pltpu

F32 = jnp.float32
BF16 = jnp.bfloat16

A_HEAD_DIM = 128
B_HEAD_DIM = 64
LN_EPS = 1e-5
RMS_EPS = 1e-6
N_MOD = 6

LANES = 128
VMEM_LIMIT_BYTES = 56 * 1024 * 1024

TS_INPROJ = 512
TT_HGRN = 256
HGRN_LEAF = 16
BQ_ATT = 512
BK_ATT = 256
TM_OUT = 512
TM_FFN = 512
FFN_CHUNK = 1536
ROW_SPLIT = 2
NEG_BIG = -1e30

_CQ_LANES = (64, 65, 66)
_CK_LANES = (67, 68, 69)
V_ROWS = 80
_AUG_GROUP = 16
LOG2E = 1.4426950408889634


def _cparams(n_axes):
    return pltpu.CompilerParams(dimension_semantics=("arbitrary",) * n_axes,
                                vmem_limit_bytes=VMEM_LIMIT_BYTES)


def _const_spec(shape):
    nd = len(shape)
    return pl.BlockSpec(shape, lambda *_: (0,) * nd, pipeline_mode=pl.Buffered(1))


def _split3(x):
    hi = x.astype(BF16)
    r1 = x - hi.astype(F32)
    mid = r1.astype(BF16)
    lo = (r1 - mid.astype(F32)).astype(BF16)
    return hi, mid, lo


def _layer_norm(z, w, b):
    mu = jnp.mean(z, axis=-1, keepdims=True)
    zc = z - mu
    var = jnp.mean(zc * zc, axis=-1, keepdims=True)
    return zc * lax.rsqrt(var + LN_EPS) * w + b


def _ada_body(c_ref, w_ref, b_ref, o_ref):
    c = c_ref[...]
    act = c * jax.nn.sigmoid(c)
    o_ref[...] = jnp.dot(act.astype(BF16), w_ref[...].astype(BF16), preferred_element_type=F32) + b_ref[...]


def _ada_mod(c, w, b):
    bsz, d = c.shape
    n = w.shape[1]
    tn = n // 4
    return pl.pallas_call(
        _ada_body,
        grid=(n // tn,),
        in_specs=[pl.BlockSpec((bsz, d), lambda j: (0, 0)),
                  pl.BlockSpec((d, tn), lambda j: (0, j)),
                  pl.BlockSpec((1, tn), lambda j: (0, j))],
        out_specs=pl.BlockSpec((bsz, tn), lambda j: (0, j)),
        out_shape=jax.ShapeDtypeStruct((bsz, n), F32),
        compiler_params=_cparams(1),
        name="ada_mod",
    )(c, w, b)


def _fox_aug_selector(heads):
    assert _AUG_GROUP * heads <= LANES and 3 * heads < LANES
    sel = np.zeros((LANES, 2 * LANES), np.float32)
    one_row = LANES - 1
    for h in range(heads):
        for part in range(3):
            sel[part * heads + h, _AUG_GROUP * h + part] = 1.0
            sel[one_row, _AUG_GROUP * h + 3 + part] = 1.0
            sel[one_row, LANES + _AUG_GROUP * h + part] = 1.0
            sel[part * heads + h, LANES + _AUG_GROUP * h + 3 + part] = -1.0
    return sel


def _inproj_body(x_ref, mod_ref, wa_ref, wb_ref, wf_ref, wg_ref, fb_ref, lbl_ref, sel_ref,
                 aq_ref, ak_ref, alf_ref, av_ref, aog_ref, qa_ref, ka_ref, vta_ref, gate_ref,
                 carry_ref, *, a_width, b_width, scale):
    @pl.when(pl.program_id(1) == 0)
    def _():
        carry_ref[...] = jnp.zeros_like(carry_ref)

    x = x_ref[0]
    sh = mod_ref[0, 0:1, :]
    sc = mod_ref[0, 1:2, :]
    h = (x * (1.0 + sc) + sh).astype(BF16)

    aw = a_width
    pa = jnp.dot(h, wa_ref[...], preferred_element_type=F32)
    l0 = lbl_ref[0:1, :]
    l1 = lbl_ref[1:2, :]
    lmax = jnp.maximum(l0, l1)
    e0 = jnp.exp(l0 - lmax)
    lb = e0 / (e0 + jnp.exp(l1 - lmax))
    f = lb + (1.0 - lb) * jax.nn.sigmoid(pa[:, aw:2 * aw])
    aq_ref[0] = pa[:, 0:aw].astype(BF16)
    ak_ref[0] = (1.0 - f).astype(BF16)
    alf_ref[0] = jnp.log(f)
    av_ref[0] = pa[:, 2 * aw:3 * aw].astype(BF16)
    aog_ref[0] = jax.nn.sigmoid(pa[:, 3 * aw:4 * aw]).astype(BF16)

    pf = jnp.dot(h, wf_ref[...], preferred_element_type=F32) + fb_ref[...]
    lsg = jnp.minimum(pf, 0.0) - jnp.log(1.0 + jnp.exp(-jnp.abs(pf)))
    ts = lsg.shape[0]
    r = lax.broadcasted_iota(jnp.int32, (ts, ts), 0)
    cidx = lax.broadcasted_iota(jnp.int32, (ts, ts), 1)
    tri = jnp.where(r >= cidx, 1.0, 0.0).astype(BF16)
    hi, mid, lo = _split3(lsg)
    parts = jnp.dot(tri, jnp.concatenate([hi, mid, lo], axis=1), preferred_element_type=F32)
    cum = parts[:, 0:LANES] + parts[:, LANES:2 * LANES] + parts[:, 2 * LANES:3 * LANES] + carry_ref[...]
    carry_ref[...] = cum[ts - 1:ts, :]

    bw = b_width
    heads = bw // B_HEAD_DIM
    pb = jnp.dot(h, wb_ref[...], preferred_element_type=F32)
    lane = lax.broadcasted_iota(jnp.int32, (ts, LANES), 1)
    c_hi, c_mid, c_lo = _split3(cum * LOG2E)
    comp = jnp.where(lane < heads, c_hi.astype(F32),
                     jnp.where(lane < 2 * heads, pltpu.roll(c_mid.astype(F32), heads, axis=1),
                               jnp.where(lane < 3 * heads, pltpu.roll(c_lo.astype(F32), 2 * heads, axis=1),
                                         jnp.where(lane == LANES - 1, 1.0, 0.0))))
    groups = jnp.dot(comp.astype(BF16), sel_ref[...], preferred_element_type=F32)
    aug_lo, aug_hi = _CQ_LANES[0], _CK_LANES[2] + 1
    for hh in range(heads):
        p, j = divmod(hh, LANES // B_HEAD_DIM)
        qx = pb[:, LANES * p:LANES * (p + 1)] * (scale * LOG2E)
        kx = pb[:, bw + LANES * p:bw + LANES * (p + 1)]
        if j:
            qx = pltpu.roll(qx, LANES - B_HEAD_DIM * j, axis=1)
            kx = pltpu.roll(kx, LANES - B_HEAD_DIM * j, axis=1)
        shift = (aug_lo - _AUG_GROUP * hh) % LANES
        gq = groups[:, 0:LANES]
        gk = groups[:, LANES:2 * LANES]
        if shift:
            gq = pltpu.roll(gq, shift, axis=1)
            gk = pltpu.roll(gk, shift, axis=1)
        qa_ref[0, hh] = jnp.where(lane < aug_lo, qx, jnp.where(lane < aug_hi, gq, 0.0)).astype(BF16)
        ka_ref[0, hh] = jnp.where(lane < aug_lo, kx, jnp.where(lane < aug_hi, gk, 0.0)).astype(BF16)
    vt = pb[:, 2 * bw:3 * bw].T
    pad_rows = vta_ref.shape[2] - B_HEAD_DIM
    one_row = jnp.where(lax.broadcasted_iota(jnp.int32, (pad_rows, ts), 0) == 0, 1.0, 0.0).astype(BF16)
    for hh in range(heads):
        vta_ref[0, hh, 0:B_HEAD_DIM, :] = vt[B_HEAD_DIM * hh:B_HEAD_DIM * (hh + 1), :].astype(BF16)
        vta_ref[0, hh, B_HEAD_DIM:B_HEAD_DIM + pad_rows, :] = one_row

    pg = jnp.dot(h, wg_ref[...], preferred_element_type=F32)
    gate_ref[0] = jax.nn.sigmoid(pg).astype(BF16)


def _inproj(x, mod3, wa, wb, wf, wg, fb, lbl, *, a_width, b_width):
    bsz, seq, d = x.shape
    ts = min(TS_INPROJ, seq)
    n_gate = wg.shape[1]
    heads = b_width // B_HEAD_DIM
    sel = jnp.asarray(_fox_aug_selector(heads), BF16)
    tok = lambda w: pl.BlockSpec((1, ts, w), lambda b, s: (b, s, 0))
    head_slot = pl.BlockSpec((1, heads, ts, LANES), lambda b, s: (b, 0, s, 0))
    out_shapes = [
        jax.ShapeDtypeStruct((bsz, seq, a_width), BF16),
        jax.ShapeDtypeStruct((bsz, seq, a_width), BF16),
        jax.ShapeDtypeStruct((bsz, seq, a_width), F32),
        jax.ShapeDtypeStruct((bsz, seq, a_width), BF16),
        jax.ShapeDtypeStruct((bsz, seq, a_width), BF16),
        jax.ShapeDtypeStruct((bsz, heads, seq, LANES), BF16),
        jax.ShapeDtypeStruct((bsz, heads, seq, LANES), BF16),
        jax.ShapeDtypeStruct((bsz, heads, V_ROWS, seq), BF16),
        jax.ShapeDtypeStruct((bsz, seq, n_gate), BF16),
    ]
    out_specs = [tok(a_width), tok(a_width), tok(a_width), tok(a_width), tok(a_width),
                 head_slot, head_slot,
                 pl.BlockSpec((1, heads, V_ROWS, ts), lambda b, s: (b, 0, 0, s)),
                 tok(n_gate)]
    body = functools.partial(_inproj_body, a_width=a_width, b_width=b_width,
                             scale=float(B_HEAD_DIM) ** -0.5)
    return pl.pallas_call(
        body,
        grid=(bsz, seq // ts),
        in_specs=[tok(d),
                  pl.BlockSpec((1, N_MOD, d), lambda b, s: (b, 0, 0)),
                  _const_spec(wa.shape), _const_spec(wb.shape), _const_spec(wf.shape),
                  _const_spec(wg.shape), _const_spec(fb.shape), _const_spec(lbl.shape),
                  _const_spec(sel.shape)],
        out_specs=out_specs,
        out_shape=out_shapes,
        scratch_shapes=[pltpu.VMEM((1, LANES), F32)],
        compiler_params=_cparams(2),
        name="in_proj",
    )(x, mod3, wa, wb, wf, wg, fb, lbl, sel)


def _hgrn_halves(tt):
    assert tt % HGRN_LEAF == 0 and (tt & (tt - 1)) == 0
    halves, m = [], tt // 2
    while m >= HGRN_LEAF:
        halves.append(m)
        m //= 2
    return tuple(halves)


def _hgrn_level_table(tt):
    halves = _hgrn_halves(tt)
    t = np.arange(tt)[:, None]
    s = np.arange(tt)[None, :]
    lvl = np.full((tt, tt), -1, np.int32)
    for i, m in enumerate(halves):
        blk = 2 * m
        own = ((t // blk) == (s // blk)) & ((t % blk) >= m) & ((s % blk) < m)
        lvl[own] = i
    leaf = ((t // HGRN_LEAF) == (s // HGRN_LEAF)) & (t >= s)
    lvl[leaf] = len(halves)
    return lvl


def _block_row(b, blk, r):
    t, l = b.shape
    x = b.reshape(t // blk, blk, l)
    g = x[:, r:r + 1, :]
    return jnp.broadcast_to(g, x.shape).reshape(t, l)


def _hgrn_body(q_ref, k_ref, lf_ref, v_ref, og_ref, nw_ref, lvl_ref, y_ref, st_ref, *, heads):
    @pl.when(pl.program_id(1) == 0)
    def _():
        st_ref[...] = jnp.zeros_like(st_ref)

    tt = q_ref.shape[1]
    hd = A_HEAD_DIM
    halves = _hgrn_halves(tt)
    lvl = lvl_ref[...]
    owned = [lvl == i for i in range(len(halves) + 1)]
    row = lax.broadcasted_iota(jnp.int32, (tt, hd), 0)
    later_half = [(row % (2 * m)) >= m for m in halves]
    sign = [jnp.where(lh, 1.0, -1.0) for lh in later_half]

    r = lax.broadcasted_iota(jnp.int32, (tt, tt), 0)
    c = lax.broadcasted_iota(jnp.int32, (tt, tt), 1)
    tri = jnp.where(r >= c, 1.0, 0.0).astype(BF16)
    hi, mid, lo = _split3(lf_ref[0])
    b_all = (jnp.dot(tri, hi, preferred_element_type=F32)
             + jnp.dot(tri, mid, preferred_element_type=F32)
             + jnp.dot(tri, lo, preferred_element_type=F32))

    nt = (((1,), (1,)), ((), ()))
    for h in range(heads):
        sl = slice(hd * h, hd * (h + 1))
        q = q_ref[0, :, sl].astype(F32)
        k = k_ref[0, :, sl].astype(F32)
        v = v_ref[0, :, sl]
        b = b_all[:, sl]

        a = jnp.zeros((tt, tt), BF16)
        for i, m in enumerate(halves):
            g = _block_row(b, 2 * m, m - 1)
            x = (jnp.where(later_half[i], q, k) * jnp.exp((b - g) * sign[i])).astype(BF16)
            a = jnp.where(owned[i], lax.dot_general(x, x, nt, preferred_element_type=F32).astype(BF16), a)
        g = _block_row(b, HGRN_LEAF, HGRN_LEAF // 2 - 1)
        e = b - g
        ql = (q * jnp.exp(e)).astype(BF16)
        kl = (k * jnp.exp(-e)).astype(BF16)
        a = jnp.where(owned[-1], lax.dot_general(ql, kl, nt, preferred_element_type=F32).astype(BF16), a)
        o_intra = jnp.dot(a, v, preferred_element_type=F32)

        st = st_ref[h]
        b_last = b[tt - 1:tt, :]
        qd = (q * jnp.exp(b)).astype(BF16)
        o = o_intra + lax.dot_general(qd, st.astype(BF16), nt, preferred_element_type=F32)
        kd = (k * jnp.exp(b_last - b)).astype(BF16)
        vt = v.astype(F32).T.astype(BF16)
        st_ref[h] = st * jnp.exp(b_last) + jnp.dot(vt, kd, preferred_element_type=F32)

        ms = jnp.mean(o * o, axis=-1, keepdims=True)
        y = o * lax.rsqrt(ms + RMS_EPS) * nw_ref[:, sl] * og_ref[0, :, sl].astype(F32)
        y_ref[0, :, sl] = y.astype(BF16)


def _hgrn(aq, ak, alf, av, aog, norm_w):
    bsz, seq, width = aq.shape
    heads = width // A_HEAD_DIM
    tt = min(TT_HGRN, seq)
    lvl = jnp.asarray(_hgrn_level_table(tt), BF16)
    tok = pl.BlockSpec((1, tt, width), lambda b, s: (b, s, 0))
    return pl.pallas_call(
        functools.partial(_hgrn_body, heads=heads),
        grid=(bsz, seq // tt),
        in_specs=[tok, tok, tok, tok, tok, _const_spec((1, width)), _const_spec((tt, tt))],
        out_specs=tok,
        out_shape=jax.ShapeDtypeStruct((bsz, seq, width), BF16),
        scratch_shapes=[pltpu.VMEM((heads, A_HEAD_DIM, A_HEAD_DIM), F32)],
        compiler_params=_cparams(2),
        name="hgrn2",
    )(aq, ak, alf, av, aog, norm_w, lvl)


def _fox_body(qa_ref, qn_ref, ka_ref, vta_ref, o_ref, s_ref, tmax_ref, m_ref, acc_ref, *, heads, bq, bk):
    qi = pl.program_id(1)
    r = bq // bk
    nt = (((1,), (1,)), ((), ()))
    m_ref[...] = jnp.full(m_ref.shape, NEG_BIG, F32)
    acc_ref[...] = jnp.zeros(acc_ref.shape, F32)

    def scores(kj, slot, h, visible=None, q_ref=qa_ref):
        k0 = pl.multiple_of(kj * bk, bk)
        s = lax.dot_general(ka_ref[0, h, pl.ds(k0, bk), :], q_ref[0, h], nt,
                            preferred_element_type=F32)
        if visible is not None:
            s = jnp.where(visible, s, NEG_BIG)
        s_ref[slot, h] = s
        tmax_ref[slot, h] = jnp.max(s, axis=0, keepdims=True)

    def softmax_accumulate(kj, slot, h, visible=None):
        s = s_ref[slot, h]
        if visible is None:
            tmax = tmax_ref[slot, h]
        else:
            s = jnp.where(visible, s, NEG_BIG)
            tmax = jnp.max(s, axis=0, keepdims=True)
        m_old = m_ref[h]
        m_new = jnp.maximum(m_old, tmax)
        alpha = jnp.exp2(m_old - m_new)
        m_ref[h] = m_new
        p = jnp.exp2(s - m_new).astype(BF16)
        k0 = pl.multiple_of(kj * bk, bk)
        vt = vta_ref[0, h, :, pl.ds(k0, bk)]
        acc_ref[h] = acc_ref[h] * alpha + jnp.dot(vt, p, preferred_element_type=F32)

    def trip(t, carry):
        for i in range(r):
            kj = t * r + i
            for h in range(heads):
                scores(kj + 1, (i + 1) % r, h)
                softmax_accumulate(kj, i, h)
        return carry

    @pl.when(qi == 0)
    def _():
        for h in range(heads):
            scores(0, 0, h)

    lax.fori_loop(0, qi, trip, 0)

    krow = lax.broadcasted_iota(jnp.int32, (bk, bq), 0)
    qcol = lax.broadcasted_iota(jnp.int32, (bk, bq), 1)
    for i in range(r):
        kj = qi * r + i
        for h in range(heads):
            if i + 1 < r:
                scores(kj + 1, i + 1, h, visible=(krow + (i + 1) * bk <= qcol))
            softmax_accumulate(kj, i, h, visible=(krow <= qcol) if i == 0 else None)
            if i + 1 == r:
                scores(0, 0, h, q_ref=qn_ref)

    outs = []
    for h in range(heads):
        a = acc_ref[h]
        outs.append(a[0:B_HEAD_DIM, :] / a[B_HEAD_DIM:B_HEAD_DIM + 1, :])
    o_ref[0] = jnp.concatenate(outs, axis=0).T.astype(BF16)


def _fox(qa, ka, vta):
    bsz, heads, seq, _ = qa.shape
    bq = min(BQ_ATT, seq)
    bk = min(BK_ATT, seq)
    assert bq % bk == 0 and bq // bk >= 2 and seq % bq == 0
    nq = seq // bq
    return pl.pallas_call(
        functools.partial(_fox_body, heads=heads, bq=bq, bk=bk),
        grid=(bsz, nq),
        in_specs=[pl.BlockSpec((1, heads, bq, LANES), lambda b, i: (b, 0, i, 0)),
                  pl.BlockSpec((1, heads, bq, LANES), lambda b, i: (b, 0, jnp.minimum(i + 1, nq - 1), 0)),
                  pl.BlockSpec((1, heads, seq, LANES), lambda b, i: (b, 0, 0, 0)),
                  pl.BlockSpec((1, heads, V_ROWS, seq), lambda b, i: (b, 0, 0, 0))],
        out_specs=pl.BlockSpec((1, bq, heads * B_HEAD_DIM), lambda b, i: (b, i, 0)),
        out_shape=jax.ShapeDtypeStruct((bsz, seq, heads * B_HEAD_DIM), BF16),
        scratch_shapes=[pltpu.VMEM((bq // bk, heads, bk, bq), F32),
                        pltpu.VMEM((bq // bk, heads, 1, bq), F32),
                        pltpu.VMEM((heads, 1, bq), F32),
                        pltpu.VMEM((heads, V_ROWS, bq), F32)],
        compiler_params=_cparams(2),
        name="fox_attn",
    )(qa, qa, ka, vta)


def _merge_body(x_ref, ya_ref, yb_ref, gate_ref, mod_ref, wa_ref, wb_ref, wo_ref, lnw_ref, lnb_ref,
                o_ref, *, alpha):
    tm, d = x_ref.shape[1], x_ref.shape[2]
    g1 = mod_ref[0, 2:3, :]
    rows = tm // ROW_SPLIT
    for part in range(ROW_SPLIT):
        rs = slice(rows * part, rows * (part + 1))
        ma = jnp.dot(ya_ref[0, rs, :], wa_ref[...], preferred_element_type=F32)
        mb = jnp.dot(yb_ref[0, rs, :], wb_ref[...], preferred_element_type=F32)
        ga = gate_ref[0, rs, 0:d].astype(F32)
        gb = gate_ref[0, rs, d:2 * d].astype(F32)
        merged = (ga * ma + gb * mb).astype(BF16)
        y = jnp.dot(merged, wo_ref[...], preferred_element_type=F32)
        z = alpha * x_ref[0, rs, :] + (1.0 + g1) * y
        o_ref[0, rs, :] = _layer_norm(z, lnw_ref[...], lnb_ref[...])


def _merge(x, ya, yb, gates, mod3, wa, wb, wo, lnw, lnb, *, alpha):
    bsz, seq, d = x.shape
    tm = min(TM_OUT, seq)
    tok = lambda w: pl.BlockSpec((1, tm, w), lambda b, s: (b, s, 0))
    return pl.pallas_call(
        functools.partial(_merge_body, alpha=alpha),
        grid=(bsz, seq // tm),
        in_specs=[tok(d), tok(ya.shape[2]), tok(yb.shape[2]), tok(gates.shape[2]),
                  pl.BlockSpec((1, N_MOD, d), lambda b, s: (b, 0, 0)),
                  _const_spec(wa.shape), _const_spec(wb.shape), _const_spec(wo.shape),
                  _const_spec(lnw.shape), _const_spec(lnb.shape)],
        out_specs=tok(d),
        out_shape=jax.ShapeDtypeStruct((bsz, seq, d), F32),
        compiler_params=_cparams(2),
        name="merge_out_ln",
    )(x, ya, yb, gates, mod3, wa, wb, wo, lnw, lnb)


def _ffn_chunks(d_ff):
    out, start = [], 0
    while start < d_ff:
        size = min(FFN_CHUNK, d_ff - start)
        out.append((start, size))
        start += size
    return out


def _ffn_body(x_ref, mod_ref, wg_ref, wu_ref, wd_ref, lnw_ref, lnb_ref, o_ref, *, alpha, chunks):
    sh = mod_ref[0, 3:4, :]
    sc = mod_ref[0, 4:5, :]
    g2 = mod_ref[0, 5:6, :]
    rows = x_ref.shape[1] // ROW_SPLIT
    for part in range(ROW_SPLIT):
        rs = slice(rows * part, rows * (part + 1))
        x = x_ref[0, rs, :]
        h = (x * (1.0 + sc) + sh).astype(BF16)
        y = jnp.zeros(x.shape, F32)
        for start, size in chunks:
            gate = jnp.dot(h, wg_ref[:, start:start + size], preferred_element_type=F32)
            up = jnp.dot(h, wu_ref[:, start:start + size], preferred_element_type=F32)
            act = (gate * jax.nn.sigmoid(gate) * up).astype(BF16)
            y = y + jnp.dot(act, wd_ref[start:start + size, :], preferred_element_type=F32)
        z = alpha * x + (1.0 + g2) * y
        o_ref[0, rs, :] = _layer_norm(z, lnw_ref[...], lnb_ref[...])


def _ffn(x, mod3, wg, wu, wd, lnw, lnb, *, alpha):
    bsz, seq, d = x.shape
    tm = min(TM_FFN, seq)
    tok = pl.BlockSpec((1, tm, d), lambda b, s: (b, s, 0))
    body = functools.partial(_ffn_body, alpha=alpha, chunks=_ffn_chunks(wg.shape[1]))
    return pl.pallas_call(
        body,
        grid=(bsz, seq // tm),
        in_specs=[tok, pl.BlockSpec((1, N_MOD, d), lambda b, s: (b, 0, 0)),
                  _const_spec(wg.shape), _const_spec(wu.shape), _const_spec(wd.shape),
                  _const_spec(lnw.shape), _const_spec(lnb.shape)],
        out_specs=tok,
        out_shape=jax.ShapeDtypeStruct((bsz, seq, d), F32),
        compiler_params=_cparams(2),
        name="ffn_ln",
    )(x, mod3, wg, wu, wd, lnw, lnb)


def kernel(x, c, w_ada, b_ada, w_in, fox_f_bias, lb_logits, hgrn_norm_w, w_branch_a, w_branch_b,
           w_out, ln1_w, ln1_b, w_ffn_gate, w_ffn_up, w_ffn_down, ln2_w, ln2_b):
    depth = w_ada.shape[0]
    assert depth == 1 and lb_logits.shape[0] == 2
    bsz, seq, d = x.shape
    a_width = w_branch_a.shape[1]
    b_width = w_branch_b.shape[1]
    b_heads = fox_f_bias.shape[1]
    alpha = (2.0 * depth) ** 0.25
    assert a_width % A_HEAD_DIM == 0 and b_width == b_heads * B_HEAD_DIM and b_heads <= LANES

    o_b = 4 * a_width
    o_f = o_b + 3 * b_width
    o_g = o_f + b_heads
    w_in0 = w_in[0]
    wa = w_in0[:, 0:o_b].astype(BF16)
    wb = w_in0[:, o_b:o_f].astype(BF16)
    wf = jnp.pad(w_in0[:, o_f:o_g], ((0, 0), (0, LANES - b_heads))).astype(BF16)
    wg = w_in0[:, o_g:].astype(BF16)
    fb = jnp.pad(fox_f_bias[0], (0, LANES - b_heads)).reshape(1, LANES)

    mod = _ada_mod(c, w_ada[0], b_ada[0].reshape(1, -1))
    mod3 = mod.reshape(bsz, N_MOD, d)

    aq, ak, alf, av, aog, qa, ka, vta, gates = _inproj(
        x, mod3, wa, wb, wf, wg, fb, lb_logits, a_width=a_width, b_width=b_width)

    ya = _hgrn(aq, ak, alf, av, aog, hgrn_norm_w[0].reshape(1, -1))
    yb = _fox(qa, ka, vta)

    x1 = _merge(x, ya, yb, gates, mod3,
                w_branch_a[0].astype(BF16), w_branch_b[0].astype(BF16), w_out[0].astype(BF16),
                ln1_w[0].reshape(1, -1), ln1_b[0].reshape(1, -1), alpha=alpha)
    return _ffn(x1, mod3, w_ffn_gate[0].astype(BF16), w_ffn_up[0].astype(BF16),
                w_ffn_down[0].astype(BF16), ln2_w[0].reshape(1, -1), ln2_b[0].reshape(1, -1),
                alpha=alpha)
```
